```python
import math
import jax, jax.numpy as jnp
from jax import lax
import numpy as np

D_MODEL = 4096
BATCH = 8
SEQ = 2048
DEPTH = 2

HEAD_DIM = 128
FOX_HEADS = 16
DSA_HEADS = 16
KV_RANK = 256
IDX_HEADS = 32
IDX_DIM = 64
TOPK_MAX = 256
Q_BLOCK = 128
CONV_WIDTH = 3
N_BUCKETS = 32
MAX_DISTANCE = 128
N_EXPERTS = 16
N_GROUPS = 4
EXPERTS_PER_GROUP = N_EXPERTS // N_GROUPS
TOP_K = 2
D_EXPERT = 1024
EXPERT_BLOCK = 256
EPS = 1e-6
FOX_W = FOX_HEADS * HEAD_DIM
DSA_W = DSA_HEADS * HEAD_DIM
EVEN_SIZES = (FOX_W, FOX_W, FOX_W, FOX_HEADS, DSA_W, KV_RANK, IDX_HEADS * IDX_DIM, IDX_DIM, IDX_HEADS)
EVEN_COLS = 3 * FOX_W + FOX_HEADS + DSA_W + KV_RANK + IDX_HEADS * IDX_DIM + IDX_DIM + IDX_HEADS
N_EVEN = (DEPTH + 1) // 2
N_ODD = DEPTH // 2

kernel_name = 'hybrid_fox_dsa_shortconv_groupmoe_adaln'


def _rmsnorm(x, g):
    xf = x.astype(jnp.float32)
    y = xf * lax.rsqrt(jnp.mean(xf * xf, axis=-1, keepdims=True) + EPS)
    return (y * g.astype(jnp.float32)).astype(x.dtype)


def _split(z, sizes):
    outs, o = [], 0
    for s in sizes:
        outs.append(z[..., o:o + s])
        o += s
    return outs


def _to_blocks(a):
    b, s = a.shape[:2]
    return a.reshape((b, s // Q_BLOCK, Q_BLOCK) + a.shape[2:]).swapaxes(0, 1)


def _from_blocks(a):
    nb, b, q = a.shape[:3]
    return a.swapaxes(0, 1).reshape((b, nb * q) + a.shape[3:])


def _t5_bucket(dist):
    max_exact = N_BUCKETS // 2
    d = jnp.maximum(dist, 0)
    df = jnp.maximum(d, 1).astype(jnp.float32)
    large = max_exact + (jnp.log(df / max_exact) / math.log(MAX_DISTANCE / max_exact)
                         * (N_BUCKETS - max_exact)).astype(jnp.int32)
    large = jnp.minimum(large, N_BUCKETS - 1)
    return jnp.where(d < max_exact, d, large)


def _fox_attention(q, k, v, f_logit):
    S = q.shape[1]
    scale = HEAD_DIM ** -0.5
    cum = jnp.cumsum(jax.nn.log_sigmoid(f_logit.astype(jnp.float32)), axis=1)
    cum_k = cum.transpose(0, 2, 1)
    pos = jnp.arange(S)

    def block(args):
        qb, cb, tb = args
        s = jnp.einsum('bqhd,bkhd->bhqk', qb, k).astype(jnp.float32) * scale
        s = s + cb.transpose(0, 2, 1)[:, :, :, None] - cum_k[:, :, None, :]
        s = jnp.where(tb[:, None] >= pos[None, :], s, -jnp.inf)
        p = jax.nn.softmax(s, axis=-1).astype(v.dtype)
        return jnp.einsum('bhqk,bkhd->bqhd', p, v)

    out = lax.map(block, (_to_blocks(q), _to_blocks(cum), pos.reshape(-1, Q_BLOCK)))
    return _from_blocks(out)


def _dsa_attention(q, ckv, iq, ik, iw, w_uk, w_uv, rel_bias):
    S = q.shape[1]
    topk = min(TOPK_MAX, S // 4)
    scale = HEAD_DIM ** -0.5
    q_lat = jnp.einsum('bshd,hrd->bshr', q, w_uk)
    iw_s = iw.astype(jnp.float32) * (IDX_HEADS ** -0.5)
    pos = jnp.arange(S)

    def block(args):
        qlb, iqb, iwb, tb = args
        rel = jax.nn.relu(jnp.einsum('bqjd,bkd->bqjk', iqb, ik).astype(jnp.float32) * (IDX_DIM ** -0.5))
        score = jnp.einsum('bqjk,bqj->bqk', rel, iwb)
        score = jnp.where((tb[:, None] >= pos[None, :])[None], score, -jnp.inf)
        _, idx = lax.top_k(score, topk)
        valid = idx <= tb[None, :, None]
        c_sel = jax.vmap(lambda cb, ib: cb[ib])(ckv, idx)
        bias = rel_bias[_t5_bucket(tb[None, :, None] - idx)]
        s = jnp.einsum('bqhr,bqkr->bqhk', qlb, c_sel).astype(jnp.float32) * scale
        s = s + bias.transpose(0, 1, 3, 2).astype(jnp.float32)
        s = jnp.where(valid[:, :, None, :], s, -jnp.inf)
        p = jax.nn.softmax(s, axis=-1).astype(c_sel.dtype)
        o_lat = jnp.einsum('bqhk,bqkr->bqhr', p, c_sel)
        return jnp.einsum('bqhr,hrd->bqhd', o_lat, w_uv)

    out = lax.map(block, (_to_blocks(q_lat), _to_blocks(iq), _to_blocks(iw_s), pos.reshape(-1, Q_BLOCK)))
    return _from_blocks(out)


def _mixer_fox_dsa(h, w_in, fox_fb, kv_norm, w_uk, w_uv, w_out, rel_bias):
    B, S, _ = h.shape
    z = jnp.einsum('bsd,de->bse', h, w_in)
    qa, ka, va, fa, qb, ckv, iq, ik, iw = _split(z, EVEN_SIZES)
    hd = (B, S, FOX_HEADS, HEAD_DIM)
    o_a = _fox_attention(qa.reshape(hd), ka.reshape(hd), va.reshape(hd), fa + fox_fb)
    o_b = _dsa_attention(qb.reshape(B, S, DSA_HEADS, HEAD_DIM), _rmsnorm(ckv, kv_norm),
                         iq.reshape(B, S, IDX_HEADS, IDX_DIM), ik, iw, w_uk, w_uv, rel_bias)
    o = jnp.concatenate([o_a.reshape(B, S, FOX_W), o_b.reshape(B, S, DSA_W)], axis=-1)
    return jnp.einsum('bse,ed->bsd', o, w_out)


def _mixer_shortconv(h, w_in, conv_w, w_out):
    D = h.shape[-1]
    z = jnp.einsum('bsd,de->bse', h, w_in)
    b_gate, c_gate, u = _split(z, (D, D, D))
    y = c_gate * u
    y = lax.conv_general_dilated(y, conv_w[:, None, :], window_strides=(1,),
                                 padding=[(CONV_WIDTH - 1, 0)],
                                 dimension_numbers=('NWC', 'WIO', 'NWC'),
                                 feature_group_count=D)
    return jnp.einsum('bsd,de->bse', b_gate * y, w_out)


def _moe(h, router_w, router_b, e_gate, e_up, e_down):
    B, S, D = h.shape
    N = B * S
    t = h.reshape(N, D)
    aff = jax.nn.sigmoid(t.astype(jnp.float32) @ router_w.astype(jnp.float32))
    sel = (aff + router_b.astype(jnp.float32)).reshape(N, N_GROUPS, EXPERTS_PER_GROUP)
    gscore = lax.top_k(sel, 2)[0].sum(-1)
    grp = jnp.argmax(gscore, axis=-1)
    in_grp = jnp.take_along_axis(sel, grp[:, None, None], axis=1)[:, 0]
    _, local = lax.top_k(in_grp, TOP_K)
    eidx = grp[:, None] * EXPERTS_PER_GROUP + local
    w = jnp.take_along_axis(aff, eidx, axis=1)
    w = w / jnp.sum(w, axis=-1, keepdims=True)
    A = N * TOP_K
    flat_e = eidx.reshape(A)
    order = jnp.argsort(flat_e)
    se = flat_e[order]
    tok = order // TOP_K
    counts = jnp.bincount(flat_e, length=N_EXPERTS)
    padded = (counts + EXPERT_BLOCK - 1) // EXPERT_BLOCK * EXPERT_BLOCK
    pad_end = jnp.cumsum(padded)
    pad_start = pad_end - padded
    start = jnp.cumsum(counts) - counts
    dest = pad_start[se] + jnp.arange(A) - start[se]
    nb = (A + N_EXPERTS * (EXPERT_BLOCK - 1) + EXPERT_BLOCK - 1) // EXPERT_BLOCK
    buf = jnp.zeros((nb * EXPERT_BLOCK, D), t.dtype).at[dest].set(t[tok])
    blk_e = jnp.minimum(jnp.searchsorted(pad_end, jnp.arange(nb) * EXPERT_BLOCK, side='right'),
                        N_EXPERTS - 1)

    def run(args):
        xb, e = args
        hid = jax.nn.silu(xb @ e_gate[e]) * (xb @ e_up[e])
        return hid @ e_down[e]

    out = lax.map(run, (buf.reshape(nb, EXPERT_BLOCK, D), blk_e)).reshape(nb * EXPERT_BLOCK, D)
    contrib = out[dest] * w.reshape(A)[order][:, None].astype(out.dtype)
    y = jax.ops.segment_sum(contrib, tok, num_segments=N)
    return y.reshape(B, S, D).astype(h.dtype)


def _n(k, shape, scale):
    return jax.random.normal(k, shape, jnp.float32) * scale


def setup_inputs(seed: int = 0) -> dict:
    key = jax.random.key(seed)
    ks = jax.random.split(key, 22)
    D = D_MODEL
    return {
        'x': _n(ks[0], (BATCH, SEQ, D), 1.0),
        'c': _n(ks[1], (BATCH, D), 1.0),
        'ada_w': _n(ks[2], (DEPTH, D, 6 * D), 0.5 * D ** -0.5),
        'ada_b': _n(ks[3], (DEPTH, 6 * D), 0.02),
        'norm1': 1.0 + _n(ks[4], (DEPTH, D), 0.02),
        'norm2': 1.0 + _n(ks[5], (DEPTH, D), 0.02),
        'w_in_ab': _n(ks[6], (N_EVEN, D, EVEN_COLS), D ** -0.5),
        'fox_fb': 2.0 + _n(ks[7], (N_EVEN, FOX_HEADS), 0.1),
        'kv_norm': 1.0 + _n(ks[8], (N_EVEN, KV_RANK), 0.02),
        'w_uk': _n(ks[9], (N_EVEN, DSA_HEADS, KV_RANK, HEAD_DIM), KV_RANK ** -0.5),
        'w_uv': _n(ks[10], (N_EVEN, DSA_HEADS, KV_RANK, HEAD_DIM), KV_RANK ** -0.5),
        'w_out_ab': _n(ks[11], (N_EVEN, FOX_W + DSA_W, D), (FOX_W + DSA_W) ** -0.5),
        'w_in_c': _n(ks[12], (N_ODD, D, 3 * D), D ** -0.5),
        'conv_w': _n(ks[13], (N_ODD, CONV_WIDTH, D), CONV_WIDTH ** -0.5),
        'w_out_c': _n(ks[14], (N_ODD, D, D), D ** -0.5),
        'rel_bias': _n(ks[15], (N_BUCKETS, DSA_HEADS), 0.5),
        'router_w': _n(ks[16], (D, N_EXPERTS), D ** -0.5),
        'router_b': _n(ks[17], (N_EXPERTS,), 0.01),
        'exp_gate': _n(ks[18], (DEPTH, N_EXPERTS, D, D_EXPERT), D ** -0.5),
        'exp_up': _n(ks[19], (DEPTH, N_EXPERTS, D, D_EXPERT), D ** -0.5),
        'exp_down': _n(ks[20], (DEPTH, N_EXPERTS, D_EXPERT, D), D_EXPERT ** -0.5),
        'final_norm': 1.0 + _n(ks[21], (D,), 0.02),
    }


def reference(x, c, ada_w, ada_b, norm1, norm2, w_in_ab, fox_fb, kv_norm, w_uk, w_uv, w_out_ab,
              w_in_c, conv_w, w_out_c, rel_bias, router_w, router_b, exp_gate, exp_up, exp_down,
              final_norm):
    c_act = jax.nn.silu(c)
    for l in range(DEPTH):
        mod = c_act @ ada_w[l] + ada_b[l]
        sh1, sc1, g1, sh2, sc2, g2 = jnp.split(mod, 6, axis=-1)
        h = _rmsnorm(x, norm1[l]) * (1.0 + sc1[:, None, :]) + sh1[:, None, :]
        i = l // 2
        if l % 2 == 0:
            mix = _mixer_fox_dsa(h, w_in_ab[i], fox_fb[i], kv_norm[i], w_uk[i], w_uv[i], w_out_ab[i], rel_bias)
        else:
            mix = _mixer_shortconv(h, w_in_c[i], conv_w[i], w_out_c[i])
        x = x + g1[:, None, :] * mix
        h = _rmsnorm(x, norm2[l]) * (1.0 + sc2[:, None, :]) + sh2[:, None, :]
        x = x + g2[:, None, :] * _moe(h, router_w, router_b, exp_gate[l], exp_up[l], exp_down[l])
    return _rmsnorm(x, final_norm)
```

```python
import functools
import math

import numpy as np
import jax
import jax.numpy as jnp
from jax import lax
from jax.experimental import pallas as pl
from jax.experimental.pallas import tpu as pltpu

F32 = jnp.float32
BF16 = jnp.bfloat16
I32 = jnp.int32

HEAD_DIM = 128
IDX_DIM = 64
TOPK_MAX = 256
N_BUCKETS = 32
MAX_DISTANCE = 128
N_GROUPS = 4
TOP_K = 2
EPS = 1e-6
LANES = 128
VMEM_LIMIT = 56 * 1024 * 1024
MOE_BLOCK = 256
NT_DIMS = (((1,), (1,)), ((), ()))


def _params(sem, vmem=VMEM_LIMIT):
    return pltpu.CompilerParams(dimension_semantics=sem, vmem_limit_bytes=vmem)


def _ada_kernel(c_ref, w_ref, b_ref, o_ref):
    c = c_ref[...]
    ca = (c * jax.nn.sigmoid(c)).astype(BF16)
    o_ref[0] = jnp.dot(ca, w_ref[0].astype(BF16), preferred_element_type=F32) + b_ref[0]


def _ada_mod(c, ada_w, ada_b):
    depth, d, n6 = ada_w.shape
    b = c.shape[0]
    tn = min(512, n6)
    return pl.pallas_call(
        _ada_kernel,
        grid=(depth, n6 // tn),
        in_specs=[
            pl.BlockSpec((b, d), lambda l, j: (0, 0)),
            pl.BlockSpec((1, d, tn), lambda l, j: (l, 0, j)),
            pl.BlockSpec((1, 1, tn), lambda l, j: (l, 0, j)),
        ],
        out_specs=pl.BlockSpec((1, b, tn), lambda l, j: (l, 0, j)),
        out_shape=jax.ShapeDtypeStruct((depth, b, n6), F32),
        compiler_params=_params(("arbitrary", "arbitrary")),
        name="ada_mod",
    )(c, ada_w, ada_b.reshape(depth, 1, n6))


def _rms(x, g):
    return x * lax.rsqrt(jnp.mean(x * x, axis=-1, keepdims=True) + EPS) * g


def _norm_mod_kernel(x_ref, g_ref, sc_ref, sh_ref, o_ref):
    y = _rms(x_ref[0], g_ref[...])
    o_ref[0] = (y * (1.0 + sc_ref[0]) + sh_ref[0]).astype(o_ref.dtype)


def _norm_kernel(x_ref, g_ref, o_ref):
    o_ref[0] = _rms(x_ref[0], g_ref[...]).astype(o_ref.dtype)


def _mod_spec(d, chunk):
    return pl.BlockSpec((1, 1, d), lambda b, *_: (b, 0, chunk))


def _norm_mod(x, g, mod, sc_chunk, sh_chunk, ts=256):
    b, s, d = x.shape
    ts = min(ts, s)
    return pl.pallas_call(
        _norm_mod_kernel,
        grid=(b, s // ts),
        in_specs=[
            pl.BlockSpec((1, ts, d), lambda bi, i: (bi, i, 0)),
            pl.BlockSpec((1, d), lambda bi, i: (0, 0)),
            _mod_spec(d, sc_chunk),
            _mod_spec(d, sh_chunk),
        ],
        out_specs=pl.BlockSpec((1, ts, d), lambda bi, i: (bi, i, 0)),
        out_shape=jax.ShapeDtypeStruct((b, s, d), BF16),
        compiler_params=_params(("arbitrary", "arbitrary")),
        name="norm_mod",
    )(x, g.reshape(1, d), mod, mod)


def _final_norm(x, g, ts=256):
    b, s, d = x.shape
    ts = min(ts, s)
    return pl.pallas_call(
        _norm_kernel,
        grid=(b, s // ts),
        in_specs=[
            pl.BlockSpec((1, ts, d), lambda bi, i: (bi, i, 0)),
            pl.BlockSpec((1, d), lambda bi, i: (0, 0)),
        ],
        out_specs=pl.BlockSpec((1, ts, d), lambda bi, i: (bi, i, 0)),
        out_shape=jax.ShapeDtypeStruct((b, s, d), x.dtype),
        compiler_params=_params(("arbitrary", "arbitrary")),
        name="final_norm",
    )(x, g.reshape(1, d))


def _mm_kernel(a_ref, w_ref, o_ref):
    o_ref[0] = jnp.dot(a_ref[0], w_ref[...], preferred_element_type=F32).astype(o_ref.dtype)


def _pick_tile(n, target):
    if n <= target:
        return n
    return max(t for t in range(LANES, target + 1, LANES) if n % t == 0)


def _matmul(a, w, out_dtype, tm=1024, tn=1024):
    b, s, k = a.shape
    n = w.shape[1]
    tm, tn = min(tm, s), _pick_tile(n, tn)
    return pl.pallas_call(
        _mm_kernel,
        grid=(b, s // tm, n // tn),
        in_specs=[
            pl.BlockSpec((1, tm, k), lambda bi, i, j: (bi, i, 0)),
            pl.BlockSpec((k, tn), lambda bi, i, j: (0, j)),
        ],
        out_specs=pl.BlockSpec((1, tm, tn), lambda bi, i, j: (bi, i, j)),
        out_shape=jax.ShapeDtypeStruct((b, s, n), out_dtype),
        compiler_params=_params(("arbitrary", "arbitrary", "arbitrary")),
        name="matmul",
    )(a, w)


def _mm_res_kernel(*refs, n_pairs):
    x_ref, g_ref, o_ref = refs[2 * n_pairs:]
    acc = None
    for p in range(n_pairs):
        part = jnp.dot(refs[2 * p][0], refs[2 * p + 1][...], preferred_element_type=F32)
        acc = part if acc is None else acc + part
    o_ref[0] = x_ref[0] + g_ref[0] * acc


def _matmul_residual(pairs, x, mod, g_chunk, tm=1024, tn=512):
    b, s, d = x.shape
    tm, tn = min(tm, s), min(tn, d)
    in_specs, args = [], []
    for a, w in pairs:
        k = a.shape[2]
        in_specs += [pl.BlockSpec((1, tm, k), lambda bi, i, j: (bi, i, 0)),
                     pl.BlockSpec((k, tn), lambda bi, i, j: (0, j))]
        args += [a, w]
    nj = d // tn
    in_specs += [pl.BlockSpec((1, tm, tn), lambda bi, i, j: (bi, i, j)),
                 pl.BlockSpec((1, 1, tn), lambda bi, i, j: (bi, 0, g_chunk * nj + j))]
    return pl.pallas_call(
        functools.partial(_mm_res_kernel, n_pairs=len(pairs)),
        grid=(b, s // tm, nj),
        in_specs=in_specs,
        out_specs=pl.BlockSpec((1, tm, tn), lambda bi, i, j: (bi, i, j)),
        out_shape=jax.ShapeDtypeStruct((b, s, d), x.dtype),
        compiler_params=_params(("arbitrary", "arbitrary", "arbitrary")),
        name="matmul_residual",
    )(*args, x, mod)


def _lane_cumsum(x):
    n = x.shape[-1]
    lane = lax.broadcasted_iota(I32, x.shape, x.ndim - 1)
    d = 1
    while d < n:
        x = x + jnp.where(lane >= d, pltpu.roll(x, d, x.ndim - 1), 0.0)
        d *= 2
    return x


def _even_prep_kernel(z_ref, kvn_ref, fb_ref, ckv_ref, negcum_ref, ike_ref, iko_ref, iw_ref,
                      *, rank, fox_heads, idx_heads):
    z = z_ref[0]
    ckv_ref[0] = _rms(z[:, :rank], kvn_ref[...]).astype(BF16)
    f_t = z[:, rank:rank + LANES].T
    logf = jax.nn.log_sigmoid(f_t[:fox_heads] + fb_ref[...])
    negcum_ref[0] = -_lane_cumsum(logf)
    ik = z[:, rank + LANES:rank + LANES + IDX_DIM].astype(BF16)
    zero = jnp.zeros_like(ik)
    ike_ref[0] = jnp.concatenate([ik, zero], axis=1)
    iko_ref[0] = jnp.concatenate([zero, ik], axis=1)
    iw_ref[0] = z[:, rank + 2 * LANES:rank + 2 * LANES + idx_heads] * (idx_heads ** -0.5) * (IDX_DIM ** -0.5)


def _even_prep(z_s, kv_norm, fox_fb, idx_heads):
    b, s, w = z_s.shape
    rank = kv_norm.shape[0]
    fh = fox_fb.shape[0]
    outs = (
        jax.ShapeDtypeStruct((b, s, rank), BF16),
        jax.ShapeDtypeStruct((b, fh, s), F32),
        jax.ShapeDtypeStruct((b, s, LANES), BF16),
        jax.ShapeDtypeStruct((b, s, LANES), BF16),
        jax.ShapeDtypeStruct((b, s, idx_heads), F32),
    )
    return pl.pallas_call(
        functools.partial(_even_prep_kernel, rank=rank, fox_heads=fh, idx_heads=idx_heads),
        grid=(b,),
        in_specs=[
            pl.BlockSpec((1, s, w), lambda bi: (bi, 0, 0)),
            pl.BlockSpec((1, rank), lambda bi: (0, 0)),
            pl.BlockSpec((fh, 1), lambda bi: (0, 0)),
        ],
        out_specs=(
            pl.BlockSpec((1, s, rank), lambda bi: (bi, 0, 0)),
            pl.BlockSpec((1, fh, s), lambda bi: (bi, 0, 0)),
            pl.BlockSpec((1, s, LANES), lambda bi: (bi, 0, 0)),
            pl.BlockSpec((1, s, LANES), lambda bi: (bi, 0, 0)),
            pl.BlockSpec((1, s, idx_heads), lambda bi: (bi, 0, 0)),
        ),
        out_shape=outs,
        compiler_params=_params(("arbitrary",)),
        name="even_prep",
    )(z_s, kv_norm.reshape(1, rank), fox_fb.reshape(fh, 1))


def _fox_kernel(q_ref, k_ref, v_ref, nc_ref, o_ref, *, heads, tq, tk):
    s_len = q_ref.shape[1]
    scale = HEAD_DIM ** -0.5
    row_in = lax.broadcasted_iota(I32, (tq, tk), 0)
    col_in = lax.broadcasted_iota(I32, (tq, tk), 1)
    for hh in range(heads):
        hs = slice(hh * HEAD_DIM, (hh + 1) * HEAD_DIM)

        def q_body(qi, _, hh=hh, hs=hs):
            q0 = pl.multiple_of(qi * tq, tq)
            q = q_ref[0, pl.ds(q0, tq), hs]

            def k_body(kj, carry):
                m, l, acc = carry
                k0 = pl.multiple_of(kj * tk, tk)
                k = k_ref[0, pl.ds(k0, tk), hs]
                v = v_ref[0, pl.ds(k0, tk), hs]
                s = lax.dot_general(q, k, NT_DIMS, preferred_element_type=F32) * scale
                s = s + nc_ref[0, hh, pl.ds(kj, 1), :]
                s = jnp.where(q0 + row_in >= k0 + col_in, s, -jnp.inf)
                m_new = jnp.maximum(m, jnp.max(s, axis=-1, keepdims=True))
                alpha = jnp.exp(m - m_new)
                p = jnp.exp(s - m_new)
                l = alpha * l + jnp.sum(p, axis=-1, keepdims=True)
                acc = alpha * acc + jnp.dot(p.astype(BF16), v, preferred_element_type=F32)
                return m_new, l, acc

            init = (jnp.full((tq, 1), -jnp.inf, F32), jnp.zeros((tq, 1), F32),
                    jnp.zeros((tq, HEAD_DIM), F32))
            n_k = (q0 + tq + tk - 1) // tk
            m, l, acc = lax.fori_loop(0, n_k, k_body, init)
            o_ref[0, pl.ds(q0, tq), hs] = (acc / l).astype(o_ref.dtype)
            return 0

        lax.fori_loop(0, s_len // tq, q_body, 0)


def _fox_attention(z_a, neg_cum, fox_heads, heads_per_step=2, tq=512, tk=512):
    b, s, _ = z_a.shape
    tq, tk = min(tq, s), min(tk, s)
    hp = heads_per_step
    wblk = hp * HEAD_DIM
    nblk = fox_heads // hp
    nc = neg_cum.reshape(b, fox_heads, s // tk, tk)
    return pl.pallas_call(
        functools.partial(_fox_kernel, heads=hp, tq=tq, tk=tk),
        grid=(b, nblk),
        in_specs=[
            pl.BlockSpec((1, s, wblk), lambda bi, h: (bi, 0, h)),
            pl.BlockSpec((1, s, wblk), lambda bi, h: (bi, 0, nblk + h)),
            pl.BlockSpec((1, s, wblk), lambda bi, h: (bi, 0, 2 * nblk + h)),
            pl.BlockSpec((1, hp, s // tk, tk), lambda bi, h: (bi, h, 0, 0)),
        ],
        out_specs=pl.BlockSpec((1, s, wblk), lambda bi, h: (bi, 0, h)),
        out_shape=jax.ShapeDtypeStruct((b, s, fox_heads * HEAD_DIM), BF16),
        compiler_params=_params(("arbitrary", "arbitrary")),
        name="fox_attention",
    )(z_a, z_a, z_a, nc)


DSA_TQ = 128
INT_MIN = int(np.iinfo(np.int32).min)


def _t5_thresholds(max_d):
    max_exact = N_BUCKETS // 2
    d = np.arange(max_d)
    df = np.maximum(d, 1).astype(np.float32)
    large = max_exact + (np.log(df / np.float32(max_exact)) / np.float32(math.log(MAX_DISTANCE / max_exact))
                         * np.float32(N_BUCKETS - max_exact)).astype(np.int32)
    table = np.where(d < max_exact, d, np.minimum(large, N_BUCKETS - 1))
    assert np.all(np.diff(table) >= 0) and table[-1] == N_BUCKETS - 1
    return [int(np.argmax(table >= b)) for b in range(N_BUCKETS)]


def _bias_tile_kernel(rb_ref, o_ref, *, tq, thresholds):
    h = pl.program_id(0)
    a = lax.broadcasted_iota(I32, (tq, 2 * tq), 0)
    c = lax.broadcasted_iota(I32, (tq, 2 * tq), 1)
    d = tq + a - c
    val = jnp.full((tq, 2 * tq), rb_ref[0, h], F32)
    for b in range(1, N_BUCKETS):
        val = jnp.where(d >= thresholds[b], rb_ref[b, h], val)
    o_ref[0] = val


def _bias_tiles(rel_bias, tq):
    heads = rel_bias.shape[1]
    thresholds = _t5_thresholds(2 * tq)
    assert thresholds[N_BUCKETS - 1] <= tq + 1
    return pl.pallas_call(
        functools.partial(_bias_tile_kernel, tq=tq, thresholds=thresholds),
        grid=(heads,),
        in_specs=[pl.BlockSpec(memory_space=pltpu.SMEM)],
        out_specs=pl.BlockSpec((1, tq, 2 * tq), lambda h: (h, 0, 0)),
        out_shape=jax.ShapeDtypeStruct((heads, tq, 2 * tq), F32),
        compiler_params=_params(("arbitrary",)),
        name="bias_tiles",
    )(rel_bias)


def _dsa_kernel(rb_ref, qb_ref, iq_ref, iw_ref, ckv_ref, ike_ref, iko_ref, wuk_ref, wuv_ref, tile_ref,
                o_ref, bias_sc, key_sc, madd_sc, qlat_sc, p_sc, o_sc,
                *, heads, idx_heads, topk, tq, group):
    i = pl.program_id(0)
    b = pl.program_id(1)
    s_len, rank = ckv_ref.shape[1], ckv_ref.shape[2]
    nkb = s_len // tq
    scale = HEAD_DIM ** -0.5

    @pl.when(b == 0)
    def _():
        def fill(h, _):
            far = jnp.full((tq, tq), rb_ref[N_BUCKETS - 1, h], F32)
            for jb in range(nkb):
                bias_sc[h, jb] = far
            bias_sc[h, i] = tile_ref[h, :, tq:]

            @pl.when(i > 0)
            def _():
                bias_sc[h, i - 1] = tile_ref[h, :, :tq]
            return 0
        lax.fori_loop(0, heads, fill, 0)

    iq = iq_ref[0]
    iw = iw_ref[0]
    ike = ike_ref[0]
    iko = iko_ref[0]
    nch = idx_heads * IDX_DIM // LANES
    score = jnp.zeros((tq, s_len), F32)
    for g0 in range(0, nch, 4):
        cs = list(range(g0, min(g0 + 4, nch)))
        a = jnp.concatenate([iq[:, c * LANES:(c + 1) * LANES] for c in cs], axis=0)
        r_e = lax.dot_general(a, ike, NT_DIMS, preferred_element_type=F32)
        r_o = lax.dot_general(a, iko, NT_DIMS, preferred_element_type=F32)
        for n, c in enumerate(cs):
            rows = slice(n * tq, (n + 1) * tq)
            score = score + jnp.maximum(r_e[rows], 0.0) * iw[:, 2 * c:2 * c + 1]
            score = score + jnp.maximum(r_o[rows], 0.0) * iw[:, 2 * c + 1:2 * c + 2]

    t_idx = i * tq + lax.broadcasted_iota(I32, (tq, s_len), 0)
    s_idx = lax.broadcasted_iota(I32, (tq, s_len), 1)
    bits = pltpu.bitcast(score, I32)
    key = bits ^ ((bits >> 31) & 0x7FFFFFFF)
    key_sc[...] = jnp.where(t_idx >= s_idx, key, INT_MIN)

    def bit_step(it, lo):
        cand = lo + lax.shift_left(jnp.int32(1), 31 - it)
        cnt = jnp.sum(jnp.where(key_sc[...] >= cand, 1.0, 0.0), axis=-1, keepdims=True)
        return jnp.where(cnt >= topk, cand, lo)
    thr = lax.fori_loop(0, 32, bit_step, jnp.full((tq, 1), INT_MIN, I32))

    n_gt = jnp.sum(jnp.where(key_sc[...] > thr, 1.0, 0.0), axis=-1, keepdims=True)
    need = topk - n_gt
    nbits = s_len.bit_length() - 1

    def idx_step(it, p):
        cand = p + lax.shift_left(jnp.int32(1), nbits - 1 - it)
        tied_below = jnp.where(key_sc[...] == thr, jnp.where(s_idx < cand, 1.0, 0.0), 0.0)
        cnt = jnp.sum(tied_below, axis=-1, keepdims=True)
        return jnp.where(cnt < need, cand, p)
    p_last = lax.fori_loop(0, nbits, idx_step, jnp.zeros((tq, 1), I32))

    key = key_sc[...]
    keep = jnp.where(key > thr, 0.0, jnp.where(key == thr, jnp.where(s_idx <= p_last, 0.0, -jnp.inf), -jnp.inf))
    madd_sc[...] = jnp.where(t_idx >= s_idx, keep, -jnp.inf)

    qb = qb_ref[0]
    for h in range(heads):
        qh = qb[:, h * HEAD_DIM:(h + 1) * HEAD_DIM]
        qlat_sc[h] = lax.dot_general(qh, wuk_ref[h], NT_DIMS, preferred_element_type=F32).astype(BF16)

    ckv = ckv_ref[0]

    def group_body(g, _):
        h0 = g * group
        qg = qlat_sc[pl.ds(h0, group)].reshape(group * tq, rank)
        sg = lax.dot_general(qg, ckv, NT_DIMS, preferred_element_type=F32) * scale
        madd = madd_sc[...]
        sums = []
        for hh in range(group):
            bias = jnp.concatenate([bias_sc[h0 + hh, jb] for jb in range(nkb)], axis=1)
            s = sg[hh * tq:(hh + 1) * tq] + bias + madd
            m = jnp.max(s, axis=-1, keepdims=True)
            p = jnp.exp(s - m)
            sums.append(jnp.sum(p, axis=-1, keepdims=True))
            p_sc[hh] = p.astype(BF16)
        og = jnp.dot(p_sc[...].reshape(group * tq, s_len), ckv, preferred_element_type=F32)
        for hh in range(group):
            ol = (og[hh * tq:(hh + 1) * tq] / sums[hh]).astype(BF16)
            o_sc[h0 + hh] = jnp.dot(ol, wuv_ref[h0 + hh], preferred_element_type=F32).astype(BF16)
        return 0
    lax.fori_loop(0, heads // group, group_body, 0)

    for h in range(heads):
        o_ref[0, :, h * HEAD_DIM:(h + 1) * HEAD_DIM] = o_sc[h]


def _dsa_attention(z_a, iw_s, ckv_n, ik_e, ik_o, w_uk, w_uv, rel_bias, fox_w, idx_heads, group=4):
    b, s, _ = z_a.shape
    heads, rank, _ = w_uk.shape
    tq = min(DSA_TQ, s)
    group = min(group, heads)
    assert s & (s - 1) == 0 and heads % group == 0
    dsa_w = heads * HEAD_DIM
    iq_w = idx_heads * IDX_DIM
    assert (3 * fox_w) % dsa_w == 0 and (3 * fox_w + dsa_w) % iq_w == 0
    qb_blk = 3 * fox_w // dsa_w
    iq_blk = (3 * fox_w + dsa_w) // iq_w
    topk = min(TOPK_MAX, s // 4)
    tiles = _bias_tiles(rel_bias, tq)
    nkb = s // tq
    return pl.pallas_call(
        functools.partial(_dsa_kernel, heads=heads, idx_heads=idx_heads, topk=topk, tq=tq, group=group),
        grid=(s // tq, b),
        in_specs=[
            pl.BlockSpec(memory_space=pltpu.SMEM),
            pl.BlockSpec((1, tq, dsa_w), lambda i, bi: (bi, i, qb_blk)),
            pl.BlockSpec((1, tq, iq_w), lambda i, bi: (bi, i, iq_blk)),
            pl.BlockSpec((1, tq, idx_heads), lambda i, bi: (bi, i, 0)),
            pl.BlockSpec((1, s, rank), lambda i, bi: (bi, 0, 0)),
            pl.BlockSpec((1, s, LANES), lambda i, bi: (bi, 0, 0)),
            pl.BlockSpec((1, s, LANES), lambda i, bi: (bi, 0, 0)),
            pl.BlockSpec((heads, rank, HEAD_DIM), lambda i, bi: (0, 0, 0)),
            pl.BlockSpec((heads, rank, HEAD_DIM), lambda i, bi: (0, 0, 0)),
            pl.BlockSpec((heads, tq, 2 * tq), lambda i, bi: (0, 0, 0)),
        ],
        out_specs=pl.BlockSpec((1, tq, dsa_w), lambda i, bi: (bi, i, 0)),
        out_shape=jax.ShapeDtypeStruct((b, s, dsa_w), BF16),
        scratch_shapes=[
            pltpu.VMEM((heads, nkb, tq, tq), F32),
            pltpu.VMEM((tq, s), I32),
            pltpu.VMEM((tq, s), F32),
            pltpu.VMEM((heads, tq, rank), BF16),
            pltpu.VMEM((group, tq, s), BF16),
            pltpu.VMEM((heads, tq, HEAD_DIM), BF16),
        ],
        compiler_params=_params(("arbitrary", "arbitrary")),
        name="dsa_attention",
    )(rel_bias, z_a, z_a, iw_s, ckv_n, ik_e, ik_o, w_uk, w_uv, tiles)


def _shortconv_kernel(a_ref, wb_ref, wc_ref, wu_ref, cw_ref, o_ref):
    a = a_ref[0]
    bg = jnp.dot(a, wb_ref[...], preferred_element_type=F32)
    cg = jnp.dot(a, wc_ref[...], preferred_element_type=F32)
    u = jnp.dot(a, wu_ref[...], preferred_element_type=F32)
    p = cg * u
    row = lax.broadcasted_iota(I32, p.shape, 0)
    p1 = jnp.where(row >= 1, pltpu.roll(p, 1, 0), 0.0)
    p2 = jnp.where(row >= 2, pltpu.roll(p, 2, 0), 0.0)
    cw = cw_ref[...]
    y = p2 * cw[0:1] + p1 * cw[1:2] + p * cw[2:3]
    o_ref[0] = (bg * y).astype(o_ref.dtype)


def _shortconv(h, w_in, conv_w, tn=256):
    b, s, d = h.shape
    assert conv_w.shape[0] == 3
    tn = min(tn, d)
    nj = d // tn
    return pl.pallas_call(
        _shortconv_kernel,
        grid=(b, nj),
        in_specs=[
            pl.BlockSpec((1, s, d), lambda bi, j: (bi, 0, 0), pipeline_mode=pl.Buffered(1)),
            pl.BlockSpec((d, tn), lambda bi, j: (0, j)),
            pl.BlockSpec((d, tn), lambda bi, j: (0, nj + j)),
            pl.BlockSpec((d, tn), lambda bi, j: (0, 2 * nj + j)),
            pl.BlockSpec((3, tn), lambda bi, j: (0, j)),
        ],
        out_specs=pl.BlockSpec((1, s, tn), lambda bi, j: (bi, 0, j)),
        out_shape=jax.ShapeDtypeStruct((b, s, d), BF16),
        compiler_params=_params(("arbitrary", "arbitrary")),
        name="shortconv",
    )(h, w_in, w_in, w_in, conv_w)


def _router_kernel(x_ref, g_ref, sc_ref, sh_ref, rw_ref, rb_ref, h_ref, meta_ref, wcol_ref, cnt_ref,
                   base_sc, *, n_exp):
    epg = n_exp // N_GROUPS
    first = jnp.logical_and(pl.program_id(0) == 0, pl.program_id(1) == 0)

    @pl.when(first)
    def _():
        base_sc[...] = jnp.zeros_like(base_sc)

    h = _rms(x_ref[0], g_ref[...]) * (1.0 + sc_ref[0]) + sh_ref[0]
    t, d = h.shape
    for c in range(d // LANES):
        h_ref[0, :, c, :] = h[:, c * LANES:(c + 1) * LANES]

    logits_t = lax.dot_general(rw_ref[...], h.astype(BF16), NT_DIMS, preferred_element_type=F32)
    aff = jax.nn.sigmoid(logits_t[:n_exp])
    sel = aff + rb_ref[...]
    srow = [sel[e:e + 1] for e in range(n_exp)]
    arow = [aff[e:e + 1] for e in range(n_exp)]

    gscore = []
    for g in range(N_GROUPS):
        a0, a1, a2, a3 = srow[g * epg:(g + 1) * epg]
        hi1, lo1 = jnp.maximum(a0, a1), jnp.minimum(a0, a1)
        hi2, lo2 = jnp.maximum(a2, a3), jnp.minimum(a2, a3)
        gscore.append(jnp.maximum(hi1, hi2) + jnp.maximum(jnp.minimum(hi1, hi2), jnp.maximum(lo1, lo2)))
    grp = jnp.zeros((1, t), I32)
    best = gscore[0]
    for g in range(1, N_GROUPS):
        take = gscore[g] > best
        grp = jnp.where(take, g, grp)
        best = jnp.where(take, gscore[g], best)

    def pick(rows_, idx, n):
        out = rows_[0]
        for k in range(1, n):
            out = jnp.where(idx == k, rows_[k], out)
        return out

    in_s = [pick([srow[g * epg + k] for g in range(N_GROUPS)], grp, N_GROUPS) for k in range(epg)]
    in_a = [pick([arow[g * epg + k] for g in range(N_GROUPS)], grp, N_GROUPS) for k in range(epg)]

    i1 = jnp.zeros((1, t), I32)
    v1 = in_s[0]
    for k in range(1, epg):
        take = in_s[k] > v1
        i1 = jnp.where(take, k, i1)
        v1 = jnp.where(take, in_s[k], v1)
    i2 = jnp.where(i1 == 0, 1, 0)
    v2 = jnp.where(i1 == 0, in_s[1], in_s[0])
    for k in range(1, epg):
        take = jnp.logical_and(i1 != k, jnp.logical_and(i2 != k, in_s[k] > v2))
        i2 = jnp.where(take, k, i2)
        v2 = jnp.where(take, in_s[k], v2)
    e1 = grp * epg + i1
    e2 = grp * epg + i2
    a1 = pick(in_a, i1, epg)
    a2 = pick(in_a, i2, epg)
    denom = a1 + a2
    w1 = a1 / denom
    w2 = a2 / denom

    eiota = lax.broadcasted_iota(I32, (n_exp, t), 0)
    oh1 = jnp.where(eiota == e1, 1.0, 0.0)
    oh2 = jnp.where(eiota == e2, 1.0, 0.0)
    oh = oh1 + oh2
    before = lax.broadcasted_iota(I32, (t, t), 0) < lax.broadcasted_iota(I32, (t, t), 1)
    excl = jnp.dot(oh.astype(BF16), jnp.where(before, 1.0, 0.0).astype(BF16), preferred_element_type=F32)
    rank = base_sc[:, 0:1] + excl
    r1 = jnp.sum(oh1 * rank, axis=0, keepdims=True).astype(I32)
    r2 = jnp.sum(oh2 * rank, axis=0, keepdims=True).astype(I32)
    base_sc[...] = base_sc[...] + jnp.sum(oh, axis=1, keepdims=True)

    meta_ref[...] = jnp.concatenate([e1, e2, r1, r2, jnp.zeros((4, t), I32)], axis=0)
    wrows = jnp.concatenate([w1, w2, jnp.zeros((LANES - 2, t), F32)], axis=0)
    wcol_ref[...] = wrows.T
    cnt_ref[...] = base_sc[...].astype(I32)


def _router(x, g, mod, sc_chunk, sh_chunk, rw_pad, router_b, tt=256):
    b, s, d = x.shape
    n_exp = router_b.shape[0]
    assert n_exp // N_GROUPS == 4 and TOP_K == 2
    tt = min(tt, s)
    nt = s // tt
    n = b * s
    outs = (
        jax.ShapeDtypeStruct((b, s, d // LANES, LANES), F32),
        jax.ShapeDtypeStruct((8, n), I32),
        jax.ShapeDtypeStruct((n, LANES), F32),
        jax.ShapeDtypeStruct((n_exp, LANES), I32),
    )
    return pl.pallas_call(
        functools.partial(_router_kernel, n_exp=n_exp),
        grid=(b, nt),
        in_specs=[
            pl.BlockSpec((1, tt, d), lambda bi, i: (bi, i, 0)),
            pl.BlockSpec((1, d), lambda bi, i: (0, 0)),
            _mod_spec(d, sc_chunk),
            _mod_spec(d, sh_chunk),
            pl.BlockSpec((LANES, d), lambda bi, i: (0, 0)),
            pl.BlockSpec((n_exp, 1), lambda bi, i: (0, 0)),
        ],
        out_specs=(
            pl.BlockSpec((1, tt, d // LANES, LANES), lambda bi, i: (bi, i, 0, 0)),
            pl.BlockSpec((8, tt), lambda bi, i: (0, bi * nt + i)),
            pl.BlockSpec((tt, LANES), lambda bi, i: (bi * nt + i, 0)),
            pl.BlockSpec((n_exp, LANES), lambda bi, i: (0, 0)),
        ),
        out_shape=outs,
        scratch_shapes=[pltpu.VMEM((n_exp, LANES), F32)],
        compiler_params=_params(("arbitrary", "arbitrary")),
        name="router",
    )(x, g.reshape(1, d), mod, mod, rw_pad, router_b.reshape(n_exp, 1))


def _dispatch_kernel(e_ref, r_ref, start_ref, fill_lo_ref, fill_hi_ref, h_hbm, buf_hbm, zero_sc, sem,
                     *, chunk, n_exp):
    n_tok = h_hbm.shape[0]
    ci = pl.program_id(0)

    @pl.when(ci == 0)
    def _():
        zero_sc[...] = jnp.zeros_like(zero_sc)
        for e in range(n_exp + 1):
            lo, hi = fill_lo_ref[e], fill_hi_ref[e]

            def zfill(r, _):
                pltpu.make_async_copy(zero_sc, buf_hbm.at[r], sem).start()
                return 0
            lax.fori_loop(lo, hi, zfill, 0)

            def zdrain(r, _):
                pltpu.make_async_copy(zero_sc, buf_hbm.at[0], sem).wait()
                return 0
            lax.fori_loop(lo, hi, zdrain, 0)

    def issue(t, _):
        n = ci * chunk + t
        for k in range(TOP_K):
            a = k * n_tok + n
            pltpu.make_async_copy(h_hbm.at[n], buf_hbm.at[start_ref[e_ref[a]] + r_ref[a]], sem).start()
        return 0
    lax.fori_loop(0, chunk, issue, 0)

    def drain(t, _):
        for k in range(TOP_K):
            pltpu.make_async_copy(h_hbm.at[0], buf_hbm.at[0], sem).wait()
        return 0
    lax.fori_loop(0, chunk, drain, 0)


def _dispatch(h_slabs, e_flat, r_flat, start, fill_lo, fill_hi, n_rows, chunk=512):
    n, nc, _ = h_slabs.shape
    n_exp = start.shape[0]
    chunk = min(chunk, n)
    return pl.pallas_call(
        functools.partial(_dispatch_kernel, chunk=chunk, n_exp=n_exp),
        grid_spec=pltpu.PrefetchScalarGridSpec(
            num_scalar_prefetch=5,
            grid=(n // chunk,),
            in_specs=[pl.BlockSpec(memory_space=pl.ANY)],
            out_specs=pl.BlockSpec(memory_space=pl.ANY),
            scratch_shapes=[pltpu.VMEM((nc, LANES), h_slabs.dtype), pltpu.SemaphoreType.DMA(())],
        ),
        out_shape=jax.ShapeDtypeStruct((n_rows, nc, LANES), h_slabs.dtype),
        compiler_params=_params(("arbitrary",)),
        name="dispatch",
    )(e_flat, r_flat, start, fill_lo, fill_hi, h_slabs)


def _expert_kernel(blk_e_ref, nused_ref, x_ref, wg_ref, wu_ref, wd_ref, o_ref, xs_sc):
    @pl.when(pl.program_id(0) < nused_ref[0])
    def _():
        nc = x_ref.shape[1]
        for c in range(nc):
            xs_sc[:, c * LANES:(c + 1) * LANES] = x_ref[:, c, :].astype(BF16)
        xs = xs_sc[...]
        gate = jnp.dot(xs, wg_ref[0], preferred_element_type=F32)
        up = jnp.dot(xs, wu_ref[0], preferred_element_type=F32)
        hid = (jax.nn.silu(gate) * up).astype(BF16)
        out = jnp.dot(hid, wd_ref[0], preferred_element_type=F32)
        for c in range(nc):
            o_ref[:, c, :] = out[:, c * LANES:(c + 1) * LANES]

    @pl.when(pl.program_id(0) >= nused_ref[0])
    def _():
        o_ref[...] = jnp.zeros_like(o_ref)


def _experts(buf, blk_e, n_used, w_gate, w_up, w_down):
    n_rows, nc, _ = buf.shape
    n_exp, d, f = w_gate.shape
    nb = n_rows // MOE_BLOCK

    def row_map(bi, blk_e_ref, nused_ref):
        return (jnp.minimum(bi, nused_ref[0] - 1), 0, 0)

    def w_map(bi, blk_e_ref, nused_ref):
        return (blk_e_ref[bi], 0, 0)

    return pl.pallas_call(
        _expert_kernel,
        grid_spec=pltpu.PrefetchScalarGridSpec(
            num_scalar_prefetch=2,
            grid=(nb,),
            in_specs=[
                pl.BlockSpec((MOE_BLOCK, nc, LANES), row_map),
                pl.BlockSpec((1, d, f), w_map, pipeline_mode=pl.Buffered(1)),
                pl.BlockSpec((1, d, f), w_map, pipeline_mode=pl.Buffered(1)),
                pl.BlockSpec((1, f, d), w_map, pipeline_mode=pl.Buffered(1)),
            ],
            out_specs=pl.BlockSpec((MOE_BLOCK, nc, LANES), lambda bi, *_: (bi, 0, 0)),
            scratch_shapes=[pltpu.VMEM((MOE_BLOCK, d), BF16)],
        ),
        out_shape=jax.ShapeDtypeStruct((n_rows, nc, LANES), F32),
        compiler_params=_params(("arbitrary",)),
        name="experts",
    )(blk_e, n_used, buf, w_gate, w_up, w_down)


def _combine_kernel(e_ref, r_ref, start_ref, x_ref, g_ref, wcol_ref, ob_hbm, o_ref, gbuf, sem, *, nt, n_tok):
    tt = x_ref.shape[1]
    base = (pl.program_id(0) * nt + pl.program_id(1)) * tt

    def issue(t, _):
        for k in range(TOP_K):
            a = k * n_tok + base + t
            pltpu.make_async_copy(ob_hbm.at[start_ref[e_ref[a]] + r_ref[a]], gbuf.at[k, t], sem).start()
        return 0
    lax.fori_loop(0, tt, issue, 0)

    def drain(t, _):
        for k in range(TOP_K):
            pltpu.make_async_copy(ob_hbm.at[0], gbuf.at[0, 0], sem).wait()
        return 0
    lax.fori_loop(0, tt, drain, 0)

    w1 = wcol_ref[:, 0:1]
    w2 = wcol_ref[:, 1:2]
    for c in range(gbuf.shape[2]):
        cols = slice(c * LANES, (c + 1) * LANES)
        y = gbuf[0, :, c, :] * w1 + gbuf[1, :, c, :] * w2
        o_ref[0, :, cols] = x_ref[0, :, cols] + g_ref[0, :, cols] * y


def _combine(x, mod, g_chunk, wcol, out_buf, e_flat, r_flat, start, tt=256):
    b, s, d = x.shape
    tt = min(tt, s)
    nt = s // tt
    nc = d // LANES
    return pl.pallas_call(
        functools.partial(_combine_kernel, nt=nt, n_tok=b * s),
        grid_spec=pltpu.PrefetchScalarGridSpec(
            num_scalar_prefetch=3,
            grid=(b, nt),
            in_specs=[
                pl.BlockSpec((1, tt, d), lambda bi, i, *_: (bi, i, 0)),
                pl.BlockSpec((1, 1, d), lambda bi, i, *_: (bi, 0, g_chunk)),
                pl.BlockSpec((tt, LANES), lambda bi, i, *_: (bi * nt + i, 0)),
                pl.BlockSpec(memory_space=pl.ANY),
            ],
            out_specs=pl.BlockSpec((1, tt, d), lambda bi, i, *_: (bi, i, 0)),
            scratch_shapes=[pltpu.VMEM((TOP_K, tt, nc, LANES), F32), pltpu.SemaphoreType.DMA(())],
        ),
        out_shape=jax.ShapeDtypeStruct((b, s, d), x.dtype),
        compiler_params=_params(("arbitrary", "arbitrary")),
        name="combine",
    )(e_flat, r_flat, start, x, mod, wcol, out_buf)


def _moe(x, g, mod, rw_pad, router_b, w_gate, w_up, w_down):
    b, s, d = x.shape
    n = b * s
    n_exp = router_b.shape[0]
    h_slabs, meta, wcol, cnt = _router(x, g, mod, 4, 3, rw_pad, router_b)
    counts = cnt[:, 0]
    padded = (counts + MOE_BLOCK - 1) // MOE_BLOCK * MOE_BLOCK
    pad_end = jnp.cumsum(padded)
    start = pad_end - padded
    nb = (n * TOP_K + n_exp * (MOE_BLOCK - 1) + MOE_BLOCK - 1) // MOE_BLOCK
    blk_first = jnp.arange(nb, dtype=I32) * MOE_BLOCK
    blk_e = jnp.minimum(jnp.sum(blk_first[:, None] >= pad_end[None, :], axis=1), n_exp - 1).astype(I32)
    n_used = (pad_end[-1:] // MOE_BLOCK).astype(I32)
    e_flat = meta[0:2].reshape(-1)
    r_flat = meta[2:4].reshape(-1)
    n_rows = nb * MOE_BLOCK
    fill_lo = jnp.concatenate([start + counts, pad_end[-1:]]).astype(I32)
    fill_hi = jnp.concatenate([pad_end, jnp.full((1,), n_rows, pad_end.dtype)]).astype(I32)
    buf = _dispatch(h_slabs.reshape(n, d // LANES, LANES), e_flat, r_flat, start.astype(I32),
                    fill_lo, fill_hi, n_rows)
    out_buf = _experts(buf, blk_e, n_used, w_gate, w_up, w_down)
    return _combine(x, mod, 5, wcol, out_buf, e_flat, r_flat, start.astype(I32))


def _pad_cols(w, width):
    return jnp.pad(w, ((0, 0), (0, width - w.shape[1])))


def kernel(x, c, ada_w, ada_b, norm1, norm2, w_in_ab, fox_fb, kv_norm, w_uk, w_uv, w_out_ab, w_in_c, conv_w, w_out_c, rel_bias, router_w, router_b, exp_gate, exp_up, exp_down, final_norm):
    depth, d = norm1.shape
    b, s, _ = x.shape
    fox_heads = fox_fb.shape[1]
    dsa_heads, rank = w_uk.shape[1], w_uk.shape[2]
    fox_w, dsa_w = fox_heads * HEAD_DIM, dsa_heads * HEAD_DIM
    even_cols = w_in_ab.shape[2]
    idx_heads = (even_cols - 3 * fox_w - fox_heads - dsa_w - rank - IDX_DIM) // (IDX_DIM + 1)
    iq_w = idx_heads * IDX_DIM
    o_fa = 3 * fox_w
    o_qb = o_fa + fox_heads
    o_ckv = o_qb + dsa_w
    o_iq = o_ckv + rank
    o_ik = o_iq + iq_w
    o_iw = o_ik + IDX_DIM
    assert o_iw + idx_heads == even_cols and fox_heads <= LANES and idx_heads <= LANES

    mods = _ada_mod(c, ada_w, ada_b)
    rw_pad = _pad_cols(router_w, LANES).T.astype(BF16)
    for l in range(depth):
        mod = mods[l].reshape(b, 1, 6 * d)
        i = l // 2
        h = _norm_mod(x, norm1[l], mod, 1, 0)
        if l % 2 == 0:
            w = w_in_ab[i]
            w_a = jnp.concatenate([w[:, :o_fa], w[:, o_qb:o_ckv], w[:, o_iq:o_ik]], axis=1).astype(BF16)
            w_s = jnp.concatenate([w[:, o_ckv:o_iq], _pad_cols(w[:, o_fa:o_qb], LANES),
                                   _pad_cols(w[:, o_ik:o_iw], LANES), _pad_cols(w[:, o_iw:], LANES)],
                                  axis=1).astype(BF16)
            z_a = _matmul(h, w_a, BF16)
            z_s = _matmul(h, w_s, F32)
            ckv_n, neg_cum, ik_e, ik_o, iw_s = _even_prep(z_s, kv_norm[i], fox_fb[i], idx_heads)
            o_a = _fox_attention(z_a, neg_cum, fox_heads)
            o_b = _dsa_attention(z_a, iw_s, ckv_n, ik_e, ik_o, w_uk[i].astype(BF16), w_uv[i].astype(BF16),
                                 rel_bias, fox_w, idx_heads)
            w_o = w_out_ab[i].astype(BF16)
            x = _matmul_residual([(o_a, w_o[:fox_w]), (o_b, w_o[fox_w:])], x, mod, 2)
        else:
            y = _shortconv(h, w_in_c[i].astype(BF16), conv_w[i])
            x = _matmul_residual([(y, w_out_c[i].astype(BF16))], x, mod, 2)
        x = _moe(x, norm2[l], mod, rw_pad, router_b, exp_gate[l].astype(BF16), exp_up[l].astype(BF16),
                 exp_down[l].astype(BF16))
    return _final_norm(x, final_norm)
```

```python
import functools
import math

import numpy as np
import jax
import jax.numpy as jnp
from jax import lax
from jax.experimental import pallas as pl
from jax.experimental.pallas import tpu as pltpu

F32 = jnp.float32
BF16 = jnp.bfloat16
I32 = jnp.int32
U32 = jnp.uint32

HEAD_DIM = 128
IDX_DIM = 64
TOPK_MAX = 256
N_BUCKETS = 32
MAX_DISTANCE = 128
N_GROUPS = 4
TOP_K = 2
EPS = 1e-6
LANES = 128
VMEM_LIMIT = 56 * 1024 * 1024
MOE_BLOCK = 256
NT_DIMS = (((1,), (1,)), ((), ()))


def _params(sem, vmem=VMEM_LIMIT):
    return pltpu.CompilerParams(dimension_semantics=sem, vmem_limit_bytes=vmem)


def _ada_kernel(c_ref, w_ref, b_ref, o_ref):
    c = c_ref[...]
    ca = (c * jax.nn.sigmoid(c)).astype(BF16)
    o_ref[0] = jnp.dot(ca, w_ref[0].astype(BF16), preferred_element_type=F32) + b_ref[0]


def _ada_mod(c, ada_w, ada_b):
    depth, d, n6 = ada_w.shape
    b = c.shape[0]
    tn = min(512, n6)
    return pl.pallas_call(
        _ada_kernel,
        grid=(depth, n6 // tn),
        in_specs=[
            pl.BlockSpec((b, d), lambda l, j: (0, 0)),
            pl.BlockSpec((1, d, tn), lambda l, j: (l, 0, j)),
            pl.BlockSpec((1, 1, tn), lambda l, j: (l, 0, j)),
        ],
        out_specs=pl.BlockSpec((1, b, tn), lambda l, j: (l, 0, j)),
        out_shape=jax.ShapeDtypeStruct((depth, b, n6), F32),
        compiler_params=_params(("arbitrary", "arbitrary")),
        name="ada_mod",
    )(c, ada_w, ada_b.reshape(depth, 1, n6))


def _rms(x, g):
    return x * lax.rsqrt(jnp.mean(x * x, axis=-1, keepdims=True) + EPS) * g


def _norm_mod_kernel(x_ref, g_ref, sc_ref, sh_ref, o_ref):
    y = _rms(x_ref[0], g_ref[...])
    o_ref[0] = (y * (1.0 + sc_ref[0]) + sh_ref[0]).astype(o_ref.dtype)


def _norm_kernel(x_ref, g_ref, o_ref):
    o_ref[0] = _rms(x_ref[0], g_ref[...]).astype(o_ref.dtype)


def _mod_spec(d, chunk):
    return pl.BlockSpec((1, 1, d), lambda b, *_: (b, 0, chunk))


def _norm_mod(x, g, mod, sc_chunk, sh_chunk, ts=256):
    b, s, d = x.shape
    ts = min(ts, s)
    return pl.pallas_call(
        _norm_mod_kernel,
        grid=(b, s // ts),
        in_specs=[
            pl.BlockSpec((1, ts, d), lambda bi, i: (bi, i, 0)),
            pl.BlockSpec((1, d), lambda bi, i: (0, 0)),
            _mod_spec(d, sc_chunk),
            _mod_spec(d, sh_chunk),
        ],
        out_specs=pl.BlockSpec((1, ts, d), lambda bi, i: (bi, i, 0)),
        out_shape=jax.ShapeDtypeStruct((b, s, d), BF16),
        compiler_params=_params(("arbitrary", "arbitrary")),
        name="norm_mod",
    )(x, g.reshape(1, d), mod, mod)


def _final_norm(x, g, ts=256):
    b, s, d = x.shape
    ts = min(ts, s)
    return pl.pallas_call(
        _norm_kernel,
        grid=(b, s // ts),
        in_specs=[
            pl.BlockSpec((1, ts, d), lambda bi, i: (bi, i, 0)),
            pl.BlockSpec((1, d), lambda bi, i: (0, 0)),
        ],
        out_specs=pl.BlockSpec((1, ts, d), lambda bi, i: (bi, i, 0)),
        out_shape=jax.ShapeDtypeStruct((b, s, d), x.dtype),
        compiler_params=_params(("arbitrary", "arbitrary")),
        name="final_norm",
    )(x, g.reshape(1, d))


def _mm_kernel(a_ref, w_ref, o_ref):
    o_ref[0] = jnp.dot(a_ref[0], w_ref[...], preferred_element_type=F32).astype(o_ref.dtype)


def _pick_tile(n, target):
    if n <= target:
        return n
    return max(t for t in range(LANES, target + 1, LANES) if n % t == 0)


def _matmul(a, w, out_dtype, tm=1024, tn=1024):
    b, s, k = a.shape
    n = w.shape[1]
    tm, tn = min(tm, s), _pick_tile(n, tn)
    return pl.pallas_call(
        _mm_kernel,
        grid=(b, s // tm, n // tn),
        in_specs=[
            pl.BlockSpec((1, tm, k), lambda bi, i, j: (bi, i, 0)),
            pl.BlockSpec((k, tn), lambda bi, i, j: (0, j)),
        ],
        out_specs=pl.BlockSpec((1, tm, tn), lambda bi, i, j: (bi, i, j)),
        out_shape=jax.ShapeDtypeStruct((b, s, n), out_dtype),
        compiler_params=_params(("arbitrary", "arbitrary", "arbitrary")),
        name="matmul",
    )(a, w)


def _mm_res_kernel(*refs, n_pairs):
    x_ref, g_ref, o_ref = refs[2 * n_pairs:]
    acc = None
    for p in range(n_pairs):
        part = jnp.dot(refs[2 * p][0], refs[2 * p + 1][...], preferred_element_type=F32)
        acc = part if acc is None else acc + part
    o_ref[0] = x_ref[0] + g_ref[0] * acc


def _matmul_residual(pairs, x, mod, g_chunk, tm=1024, tn=512):
    b, s, d = x.shape
    tm, tn = min(tm, s), min(tn, d)
    in_specs, args = [], []
    for a, w in pairs:
        k = a.shape[2]
        in_specs += [pl.BlockSpec((1, tm, k), lambda bi, i, j: (bi, i, 0)),
                     pl.BlockSpec((k, tn), lambda bi, i, j: (0, j))]
        args += [a, w]
    nj = d // tn
    in_specs += [pl.BlockSpec((1, tm, tn), lambda bi, i, j: (bi, i, j)),
                 pl.BlockSpec((1, 1, tn), lambda bi, i, j: (bi, 0, g_chunk * nj + j))]
    return pl.pallas_call(
        functools.partial(_mm_res_kernel, n_pairs=len(pairs)),
        grid=(b, s // tm, nj),
        in_specs=in_specs,
        out_specs=pl.BlockSpec((1, tm, tn), lambda bi, i, j: (bi, i, j)),
        out_shape=jax.ShapeDtypeStruct((b, s, d), x.dtype),
        compiler_params=_params(("arbitrary", "arbitrary", "arbitrary")),
        name="matmul_residual",
    )(*args, x, mod)


def _lane_cumsum(x):
    n = x.shape[-1]
    lane = lax.broadcasted_iota(I32, x.shape, x.ndim - 1)
    d = 1
    while d < n:
        x = x + jnp.where(lane >= d, pltpu.roll(x, d, x.ndim - 1), 0.0)
        d *= 2
    return x


def _even_prep_kernel(z_ref, kvn_ref, fb_ref, ckv_ref, negcum_ref, ike_ref, iko_ref, iw_ref,
                      *, rank, fox_heads, idx_heads):
    z = z_ref[0]
    ckv_ref[0] = _rms(z[:, :rank], kvn_ref[...]).astype(BF16)
    f_t = z[:, rank:rank + LANES].T
    logf = jax.nn.log_sigmoid(f_t[:fox_heads] + fb_ref[...])
    negcum_ref[0] = -_lane_cumsum(logf)
    ik = z[:, rank + LANES:rank + LANES + IDX_DIM].astype(BF16)
    zero = jnp.zeros_like(ik)
    ike_ref[0] = jnp.concatenate([ik, zero], axis=1)
    iko_ref[0] = jnp.concatenate([zero, ik], axis=1)
    iw_ref[0] = z[:, rank + 2 * LANES:rank + 2 * LANES + idx_heads] * (idx_heads ** -0.5) * (IDX_DIM ** -0.5)


def _even_prep(z_s, kv_norm, fox_fb, idx_heads):
    b, s, w = z_s.shape
    rank = kv_norm.shape[0]
    fh = fox_fb.shape[0]
    outs = (
        jax.ShapeDtypeStruct((b, s, rank), BF16),
        jax.ShapeDtypeStruct((b, fh, s), F32),
        jax.ShapeDtypeStruct((b, s, LANES), BF16),
        jax.ShapeDtypeStruct((b, s, LANES), BF16),
        jax.ShapeDtypeStruct((b, s, idx_heads), F32),
    )
    return pl.pallas_call(
        functools.partial(_even_prep_kernel, rank=rank, fox_heads=fh, idx_heads=idx_heads),
        grid=(b,),
        in_specs=[
            pl.BlockSpec((1, s, w), lambda bi: (bi, 0, 0)),
            pl.BlockSpec((1, rank), lambda bi: (0, 0)),
            pl.BlockSpec((fh, 1), lambda bi: (0, 0)),
        ],
        out_specs=(
            pl.BlockSpec((1, s, rank), lambda bi: (bi, 0, 0)),
            pl.BlockSpec((1, fh, s), lambda bi: (bi, 0, 0)),
            pl.BlockSpec((1, s, LANES), lambda bi: (bi, 0, 0)),
            pl.BlockSpec((1, s, LANES), lambda bi: (bi, 0, 0)),
            pl.BlockSpec((1, s, idx_heads), lambda bi: (bi, 0, 0)),
        ),
        out_shape=outs,
        compiler_params=_params(("arbitrary",)),
        name="even_prep",
    )(z_s, kv_norm.reshape(1, rank), fox_fb.reshape(fh, 1))


def _fox_kernel(q_ref, k_ref, v_ref, nc_ref, o_ref, *, heads, tq, tk):
    s_len = q_ref.shape[1]
    scale = HEAD_DIM ** -0.5
    row_in = lax.broadcasted_iota(I32, (tq, tk), 0)
    col_in = lax.broadcasted_iota(I32, (tq, tk), 1)
    for hh in range(heads):
        hs = slice(hh * HEAD_DIM, (hh + 1) * HEAD_DIM)

        def q_body(qi, _, hh=hh, hs=hs):
            q0 = pl.multiple_of(qi * tq, tq)
            q = q_ref[0, pl.ds(q0, tq), hs]

            def k_body(kj, carry):
                m, l, acc = carry
                k0 = pl.multiple_of(kj * tk, tk)
                k = k_ref[0, pl.ds(k0, tk), hs]
                v = v_ref[0, pl.ds(k0, tk), hs]
                s = lax.dot_general(q, k, NT_DIMS, preferred_element_type=F32) * scale
                s = s + nc_ref[0, hh, pl.ds(kj, 1), :]
                s = jnp.where(q0 + row_in >= k0 + col_in, s, -jnp.inf)
                m_new = jnp.maximum(m, jnp.max(s, axis=-1, keepdims=True))
                alpha = jnp.exp(m - m_new)
                p = jnp.exp(s - m_new)
                l = alpha * l + jnp.sum(p, axis=-1, keepdims=True)
                acc = alpha * acc + jnp.dot(p.astype(BF16), v, preferred_element_type=F32)
                return m_new, l, acc

            init = (jnp.full((tq, 1), -jnp.inf, F32), jnp.zeros((tq, 1), F32),
                    jnp.zeros((tq, HEAD_DIM), F32))
            n_k = (q0 + tq + tk - 1) // tk
            m, l, acc = lax.fori_loop(0, n_k, k_body, init)
            o_ref[0, pl.ds(q0, tq), hs] = (acc / l).astype(o_ref.dtype)
            return 0

        lax.fori_loop(0, s_len // tq, q_body, 0)


def _fox_attention(z_a, neg_cum, fox_heads, heads_per_step=2, tq=512, tk=512):
    b, s, _ = z_a.shape
    tq, tk = min(tq, s), min(tk, s)
    hp = heads_per_step
    wblk = hp * HEAD_DIM
    nblk = fox_heads // hp
    nc = neg_cum.reshape(b, fox_heads, s // tk, tk)
    return pl.pallas_call(
        functools.partial(_fox_kernel, heads=hp, tq=tq, tk=tk),
        grid=(b, nblk),
        in_specs=[
            pl.BlockSpec((1, s, wblk), lambda bi, h: (bi, 0, h)),
            pl.BlockSpec((1, s, wblk), lambda bi, h: (bi, 0, nblk + h)),
            pl.BlockSpec((1, s, wblk), lambda bi, h: (bi, 0, 2 * nblk + h)),
            pl.BlockSpec((1, hp, s // tk, tk), lambda bi, h: (bi, h, 0, 0)),
        ],
        out_specs=pl.BlockSpec((1, s, wblk), lambda bi, h: (bi, 0, h)),
        out_shape=jax.ShapeDtypeStruct((b, s, fox_heads * HEAD_DIM), BF16),
        compiler_params=_params(("arbitrary", "arbitrary")),
        name="fox_attention",
    )(z_a, z_a, z_a, nc)


DSA_TQ = 128
DSA_KEY_CLASSES = 4
INT_MIN = int(np.iinfo(np.int32).min)


def _t5_thresholds(max_d):
    max_exact = N_BUCKETS // 2
    d = np.arange(max_d)
    df = np.maximum(d, 1).astype(np.float32)
    large = max_exact + (np.log(df / np.float32(max_exact)) / np.float32(math.log(MAX_DISTANCE / max_exact))
                         * np.float32(N_BUCKETS - max_exact)).astype(np.int32)
    table = np.where(d < max_exact, d, np.minimum(large, N_BUCKETS - 1))
    assert np.all(np.diff(table) >= 0) and table[-1] == N_BUCKETS - 1
    return [int(np.argmax(table >= b)) for b in range(N_BUCKETS)]


def _bias_tile_kernel(rb_ref, o_ref, *, tq, thresholds):
    h = pl.program_id(0)
    a = lax.broadcasted_iota(I32, (tq, 2 * tq), 0)
    c = lax.broadcasted_iota(I32, (tq, 2 * tq), 1)
    d = tq + a - c
    val = jnp.full((tq, 2 * tq), rb_ref[0, h], F32)
    for b in range(1, N_BUCKETS):
        val = jnp.where(d >= thresholds[b], rb_ref[b, h], val)
    o_ref[0] = val


def _bias_tiles(rel_bias, tq):
    heads = rel_bias.shape[1]
    thresholds = _t5_thresholds(2 * tq)
    assert thresholds[N_BUCKETS - 1] <= tq + 1
    return pl.pallas_call(
        functools.partial(_bias_tile_kernel, tq=tq, thresholds=thresholds),
        grid=(heads,),
        in_specs=[pl.BlockSpec(memory_space=pltpu.SMEM)],
        out_specs=pl.BlockSpec((1, tq, 2 * tq), lambda h: (h, 0, 0)),
        out_shape=jax.ShapeDtypeStruct((heads, tq, 2 * tq), F32),
        compiler_params=_params(("arbitrary",)),
        name="bias_tiles",
    )(rel_bias)


def _dsa_kernel(rb_ref, qb_ref, iq_ref, iw_ref, ckv_ref, ike_ref, iko_ref, wuk_ref, wuv_ref, tile_ref,
                o_ref, bias_sc, key_sc, madd_sc, qlat_sc, p_sc, o_sc,
                *, heads, idx_heads, topk, tq, group, i0):
    i = i0 + pl.program_id(0)
    b = pl.program_id(1)
    s_len, rank = ckv_ref.shape[1], ckv_ref.shape[2]
    nkb = s_len // tq
    scale = HEAD_DIM ** -0.5

    @pl.when(b == 0)
    def _():
        def fill(h, _):
            far = jnp.full((tq, tq), rb_ref[N_BUCKETS - 1, h], F32)
            for jb in range(nkb):
                bias_sc[h, jb] = far
            bias_sc[h, i] = tile_ref[h, :, tq:]

            @pl.when(i > 0)
            def _():
                bias_sc[h, i - 1] = tile_ref[h, :, :tq]
            return 0
        lax.fori_loop(0, heads, fill, 0)

    iq = iq_ref[0]
    iw = iw_ref[0]
    ike = ike_ref[0]
    iko = iko_ref[0]
    nch = idx_heads * IDX_DIM // LANES
    score = jnp.zeros((tq, s_len), F32)
    for g0 in range(0, nch, 4):
        cs = list(range(g0, min(g0 + 4, nch)))
        a = jnp.concatenate([iq[:, c * LANES:(c + 1) * LANES] for c in cs], axis=0)
        r_e = lax.dot_general(a, ike, NT_DIMS, preferred_element_type=F32)
        r_o = lax.dot_general(a, iko, NT_DIMS, preferred_element_type=F32)
        for n, c in enumerate(cs):
            rows = slice(n * tq, (n + 1) * tq)
            score = score + jnp.maximum(r_e[rows], 0.0) * iw[:, 2 * c:2 * c + 1]
            score = score + jnp.maximum(r_o[rows], 0.0) * iw[:, 2 * c + 1:2 * c + 2]

    t_idx = i * tq + lax.broadcasted_iota(I32, (tq, s_len), 0)
    s_idx = lax.broadcasted_iota(I32, (tq, s_len), 1)
    bits = pltpu.bitcast(score, I32)
    key = bits ^ ((bits >> 31) & 0x7FFFFFFF)
    key_sc[...] = jnp.where(t_idx >= s_idx, key, INT_MIN)

    def bit_step(it, lo):
        cand = lo + lax.shift_left(jnp.int32(1), 31 - it)
        cnt = jnp.sum(jnp.where(key_sc[...] >= cand, 1.0, 0.0), axis=-1, keepdims=True)
        return jnp.where(cnt >= topk, cand, lo)
    thr = lax.fori_loop(0, 32, bit_step, jnp.full((tq, 1), INT_MIN, I32))

    n_gt = jnp.sum(jnp.where(key_sc[...] > thr, 1.0, 0.0), axis=-1, keepdims=True)
    n_eq = jnp.sum(jnp.where(key_sc[...] == thr, 1.0, 0.0), axis=-1, keepdims=True)
    need = topk - n_gt
    nbits = (s_len - 1).bit_length()

    def idx_step(it, p):
        cand = p + lax.shift_left(jnp.int32(1), nbits - 1 - it)
        tied_below = jnp.where(key_sc[...] == thr, jnp.where(s_idx < cand, 1.0, 0.0), 0.0)
        cnt = jnp.sum(tied_below, axis=-1, keepdims=True)
        return jnp.where(cnt < need, cand, p)

    excess = jnp.where(jnp.logical_and(n_eq > need, thr != INT_MIN), 1, 0)
    p_last = lax.cond(jnp.max(excess) > 0,
                      lambda: lax.fori_loop(0, nbits, idx_step, jnp.zeros((tq, 1), I32)),
                      lambda: jnp.full((tq, 1), s_len, I32))

    key = key_sc[...]
    keep = jnp.where(key > thr, 0.0, jnp.where(key == thr, jnp.where(s_idx <= p_last, 0.0, -jnp.inf), -jnp.inf))
    madd_sc[...] = jnp.where(t_idx >= s_idx, keep, -jnp.inf)

    qb = qb_ref[0]
    for h in range(heads):
        qh = qb[:, h * HEAD_DIM:(h + 1) * HEAD_DIM]
        qlat_sc[h] = lax.dot_general(qh, wuk_ref[h], NT_DIMS, preferred_element_type=F32).astype(BF16)

    ckv = ckv_ref[0]

    def group_body(g, _):
        h0 = g * group
        qg = qlat_sc[pl.ds(h0, group)].reshape(group * tq, rank)
        sg = lax.dot_general(qg, ckv, NT_DIMS, preferred_element_type=F32) * scale
        madd = madd_sc[...]
        sums = []
        for hh in range(group):
            bias = jnp.concatenate([bias_sc[h0 + hh, jb] for jb in range(nkb)], axis=1)
            s = sg[hh * tq:(hh + 1) * tq] + bias + madd
            m = jnp.max(s, axis=-1, keepdims=True)
            p = jnp.exp(s - m)
            sums.append(jnp.sum(p, axis=-1, keepdims=True))
            p_sc[hh] = p.astype(BF16)
        og = jnp.dot(p_sc[...].reshape(group * tq, s_len), ckv, preferred_element_type=F32)
        for hh in range(group):
            ol = (og[hh * tq:(hh + 1) * tq] / sums[hh]).astype(BF16)
            o_sc[h0 + hh] = jnp.dot(ol, wuv_ref[h0 + hh], preferred_element_type=F32).astype(BF16)
        return 0
    lax.fori_loop(0, heads // group, group_body, 0)

    for h in range(heads):
        o_ref[0, :, h * HEAD_DIM:(h + 1) * HEAD_DIM] = o_sc[h]


def _dsa_attention(z_a, iw_s, ckv_n, ik_e, ik_o, w_uk, w_uv, rel_bias, fox_w, idx_heads, group=4):
    b, s, _ = z_a.shape
    heads, rank, _ = w_uk.shape
    tq = min(DSA_TQ, s)
    group = min(group, heads)
    assert s & (s - 1) == 0 and heads % group == 0
    dsa_w = heads * HEAD_DIM
    iq_w = idx_heads * IDX_DIM
    assert (3 * fox_w) % dsa_w == 0 and (3 * fox_w + dsa_w) % iq_w == 0
    qb_blk = 3 * fox_w // dsa_w
    iq_blk = (3 * fox_w + dsa_w) // iq_w
    topk = min(TOPK_MAX, s // 4)
    tiles = _bias_tiles(rel_bias, tq)
    nq = s // tq
    n_cls = min(DSA_KEY_CLASSES, nq)
    per = nq // n_cls
    assert nq % n_cls == 0
    outs = []
    for c in range(n_cls):
        i0 = c * per
        sk = (i0 + per) * tq
        outs.append(pl.pallas_call(
            functools.partial(_dsa_kernel, heads=heads, idx_heads=idx_heads, topk=topk, tq=tq, group=group,
                              i0=i0),
            grid=(per, b),
            in_specs=[
                pl.BlockSpec(memory_space=pltpu.SMEM),
                pl.BlockSpec((1, tq, dsa_w), lambda i, bi, i0=i0: (bi, i0 + i, qb_blk)),
                pl.BlockSpec((1, tq, iq_w), lambda i, bi, i0=i0: (bi, i0 + i, iq_blk)),
                pl.BlockSpec((1, tq, idx_heads), lambda i, bi, i0=i0: (bi, i0 + i, 0)),
                pl.BlockSpec((1, sk, rank), lambda i, bi: (bi, 0, 0)),
                pl.BlockSpec((1, sk, LANES), lambda i, bi: (bi, 0, 0)),
                pl.BlockSpec((1, sk, LANES), lambda i, bi: (bi, 0, 0)),
                pl.BlockSpec((heads, rank, HEAD_DIM), lambda i, bi: (0, 0, 0)),
                pl.BlockSpec((heads, rank, HEAD_DIM), lambda i, bi: (0, 0, 0)),
                pl.BlockSpec((heads, tq, 2 * tq), lambda i, bi: (0, 0, 0)),
            ],
            out_specs=pl.BlockSpec((1, tq, dsa_w), lambda i, bi: (bi, i, 0)),
            out_shape=jax.ShapeDtypeStruct((b, per * tq, dsa_w), BF16),
            scratch_shapes=[
                pltpu.VMEM((heads, sk // tq, tq, tq), F32),
                pltpu.VMEM((tq, sk), I32),
                pltpu.VMEM((tq, sk), F32),
                pltpu.VMEM((heads, tq, rank), BF16),
                pltpu.VMEM((group, tq, sk), BF16),
                pltpu.VMEM((heads, tq, HEAD_DIM), BF16),
            ],
            compiler_params=_params(("arbitrary", "arbitrary")),
            name=f"dsa_attention_{c}",
        )(rel_bias, z_a, z_a, iw_s, ckv_n, ik_e, ik_o, w_uk, w_uv, tiles))
    return jnp.concatenate(outs, axis=1)


def _shortconv_kernel(a_ref, wb_ref, wc_ref, wu_ref, cw_ref, o_ref):
    a = a_ref[0]
    bg = jnp.dot(a, wb_ref[...], preferred_element_type=F32)
    cg = jnp.dot(a, wc_ref[...], preferred_element_type=F32)
    u = jnp.dot(a, wu_ref[...], preferred_element_type=F32)
    p = cg * u
    row = lax.broadcasted_iota(I32, p.shape, 0)
    p1 = jnp.where(row >= 1, pltpu.roll(p, 1, 0), 0.0)
    p2 = jnp.where(row >= 2, pltpu.roll(p, 2, 0), 0.0)
    cw = cw_ref[...]
    y = p2 * cw[0:1] + p1 * cw[1:2] + p * cw[2:3]
    o_ref[0] = (bg * y).astype(o_ref.dtype)


def _shortconv(h, w_in, conv_w, tn=256):
    b, s, d = h.shape
    assert conv_w.shape[0] == 3
    tn = min(tn, d)
    nj = d // tn
    return pl.pallas_call(
        _shortconv_kernel,
        grid=(b, nj),
        in_specs=[
            pl.BlockSpec((1, s, d), lambda bi, j: (bi, 0, 0), pipeline_mode=pl.Buffered(1)),
            pl.BlockSpec((d, tn), lambda bi, j: (0, j)),
            pl.BlockSpec((d, tn), lambda bi, j: (0, nj + j)),
            pl.BlockSpec((d, tn), lambda bi, j: (0, 2 * nj + j)),
            pl.BlockSpec((3, tn), lambda bi, j: (0, j)),
        ],
        out_specs=pl.BlockSpec((1, s, tn), lambda bi, j: (bi, 0, j)),
        out_shape=jax.ShapeDtypeStruct((b, s, d), BF16),
        compiler_params=_params(("arbitrary", "arbitrary")),
        name="shortconv",
    )(h, w_in, w_in, w_in, conv_w)


def _pack_bf16_pairs(xb):
    half = xb.shape[1] // 2
    bits = pltpu.bitcast(xb.astype(F32), U32)
    return (bits[:, half:] & jnp.uint32(0xFFFF0000)) | (bits[:, :half] >> 16)


def _unpack_bf16_pairs(words):
    lo = pltpu.bitcast(words << 16, F32).astype(BF16)
    hi = pltpu.bitcast(words & jnp.uint32(0xFFFF0000), F32).astype(BF16)
    return jnp.concatenate([lo, hi], axis=1)


def _router_kernel(x_ref, g_ref, sc_ref, sh_ref, rw_ref, rb_ref, h_ref, meta_ref, wcol_ref, cnt_ref,
                   base_sc, *, n_exp):
    epg = n_exp // N_GROUPS
    first = jnp.logical_and(pl.program_id(0) == 0, pl.program_id(1) == 0)

    @pl.when(first)
    def _():
        base_sc[...] = jnp.zeros_like(base_sc)

    h = _rms(x_ref[0], g_ref[...]) * (1.0 + sc_ref[0]) + sh_ref[0]
    t, d = h.shape
    hb = h.astype(BF16)
    h_ref[0] = _pack_bf16_pairs(hb)

    logits_t = lax.dot_general(rw_ref[...], hb, NT_DIMS, preferred_element_type=F32)
    aff = jax.nn.sigmoid(logits_t[:n_exp])
    sel = aff + rb_ref[...]
    srow = [sel[e:e + 1] for e in range(n_exp)]
    arow = [aff[e:e + 1] for e in range(n_exp)]

    gscore = []
    for g in range(N_GROUPS):
        a0, a1, a2, a3 = srow[g * epg:(g + 1) * epg]
        hi1, lo1 = jnp.maximum(a0, a1), jnp.minimum(a0, a1)
        hi2, lo2 = jnp.maximum(a2, a3), jnp.minimum(a2, a3)
        gscore.append(jnp.maximum(hi1, hi2) + jnp.maximum(jnp.minimum(hi1, hi2), jnp.maximum(lo1, lo2)))
    grp = jnp.zeros((1, t), I32)
    best = gscore[0]
    for g in range(1, N_GROUPS):
        take = gscore[g] > best
        grp = jnp.where(take, g, grp)
        best = jnp.where(take, gscore[g], best)

    def pick(rows_, idx, n):
        out = rows_[0]
        for k in range(1, n):
            out = jnp.where(idx == k, rows_[k], out)
        return out

    in_s = [pick([srow[g * epg + k] for g in range(N_GROUPS)], grp, N_GROUPS) for k in range(epg)]
    in_a = [pick([arow[g * epg + k] for g in range(N_GROUPS)], grp, N_GROUPS) for k in range(epg)]

    i1 = jnp.zeros((1, t), I32)
    v1 = in_s[0]
    for k in range(1, epg):
        take = in_s[k] > v1
        i1 = jnp.where(take, k, i1)
        v1 = jnp.where(take, in_s[k], v1)
    i2 = jnp.where(i1 == 0, 1, 0)
    v2 = jnp.where(i1 == 0, in_s[1], in_s[0])
    for k in range(1, epg):
        take = jnp.logical_and(i1 != k, jnp.logical_and(i2 != k, in_s[k] > v2))
        i2 = jnp.where(take, k, i2)
        v2 = jnp.where(take, in_s[k], v2)
    e1 = grp * epg + i1
    e2 = grp * epg + i2
    a1 = pick(in_a, i1, epg)
    a2 = pick(in_a, i2, epg)
    denom = a1 + a2
    w1 = a1 / denom
    w2 = a2 / denom

    eiota = lax.broadcasted_iota(I32, (n_exp, t), 0)
    oh1 = jnp.where(eiota == e1, 1.0, 0.0)
    oh2 = jnp.where(eiota == e2, 1.0, 0.0)
    oh = oh1 + oh2
    before = lax.broadcasted_iota(I32, (t, t), 0) < lax.broadcasted_iota(I32, (t, t), 1)
    excl = jnp.dot(oh.astype(BF16), jnp.where(before, 1.0, 0.0).astype(BF16), preferred_element_type=F32)
    rank = base_sc[:, 0:1] + excl
    r1 = jnp.sum(oh1 * rank, axis=0, keepdims=True).astype(I32)
    r2 = jnp.sum(oh2 * rank, axis=0, keepdims=True).astype(I32)
    base_sc[...] = base_sc[...] + jnp.sum(oh, axis=1, keepdims=True)

    meta_ref[...] = jnp.concatenate([e1, e2, r1, r2, jnp.zeros((4, t), I32)], axis=0)
    wrows = jnp.concatenate([w1, w2, jnp.zeros((LANES - 2, t), F32)], axis=0)
    wcol_ref[...] = wrows.T
    cnt_ref[...] = base_sc[...].astype(I32)


def _router(x, g, mod, sc_chunk, sh_chunk, rw_pad, router_b, tt=256):
    b, s, d = x.shape
    n_exp = router_b.shape[0]
    assert n_exp // N_GROUPS == 4 and TOP_K == 2
    tt = min(tt, s)
    nt = s // tt
    n = b * s
    outs = (
        jax.ShapeDtypeStruct((b, s, d // 2), U32),
        jax.ShapeDtypeStruct((8, n), I32),
        jax.ShapeDtypeStruct((n, LANES), F32),
        jax.ShapeDtypeStruct((n_exp, LANES), I32),
    )
    return pl.pallas_call(
        functools.partial(_router_kernel, n_exp=n_exp),
        grid=(b, nt),
        in_specs=[
            pl.BlockSpec((1, tt, d), lambda bi, i: (bi, i, 0)),
            pl.BlockSpec((1, d), lambda bi, i: (0, 0)),
            _mod_spec(d, sc_chunk),
            _mod_spec(d, sh_chunk),
            pl.BlockSpec((LANES, d), lambda bi, i: (0, 0)),
            pl.BlockSpec((n_exp, 1), lambda bi, i: (0, 0)),
        ],
        out_specs=(
            pl.BlockSpec((1, tt, d // 2), lambda bi, i: (bi, i, 0)),
            pl.BlockSpec((8, tt), lambda bi, i: (0, bi * nt + i)),
            pl.BlockSpec((tt, LANES), lambda bi, i: (bi * nt + i, 0)),
            pl.BlockSpec((n_exp, LANES), lambda bi, i: (0, 0)),
        ),
        out_shape=outs,
        scratch_shapes=[pltpu.VMEM((n_exp, LANES), F32)],
        compiler_params=_params(("arbitrary", "arbitrary")),
        name="router",
    )(x, g.reshape(1, d), mod, mod, rw_pad, router_b.reshape(n_exp, 1))


def _row(ref, r):
    return ref.at[pl.ds(r, 1), :]


def _dispatch_kernel(e_ref, r_ref, start_ref, fill_lo_ref, fill_hi_ref, h_ref, buf_hbm, zero_sc, sem,
                     *, n_tok, n_exp):
    chunk = h_ref.shape[0]
    ci = pl.program_id(0)

    @pl.when(ci == 0)
    def _():
        zero_sc[...] = jnp.zeros_like(zero_sc)
        for e in range(n_exp + 1):
            lo, hi = fill_lo_ref[e], fill_hi_ref[e]

            def zfill(r, _):
                pltpu.make_async_copy(_row(zero_sc, 0), _row(buf_hbm, r), sem).start()
                return 0
            lax.fori_loop(lo, hi, zfill, 0)

            def zdrain(r, _):
                pltpu.make_async_copy(_row(zero_sc, 0), _row(buf_hbm, 0), sem).wait()
                return 0
            lax.fori_loop(lo, hi, zdrain, 0)

    def issue(t, _):
        for k in range(TOP_K):
            a = k * n_tok + ci * chunk + t
            pltpu.make_async_copy(_row(h_ref, t), _row(buf_hbm, start_ref[e_ref[a]] + r_ref[a]), sem).start()
        return 0
    lax.fori_loop(0, chunk, issue, 0)

    def drain(t, _):
        for k in range(TOP_K):
            pltpu.make_async_copy(_row(h_ref, 0), _row(buf_hbm, 0), sem).wait()
        return 0
    lax.fori_loop(0, chunk, drain, 0)


def _dispatch(h_rows, e_flat, r_flat, start, fill_lo, fill_hi, n_rows, chunk=256):
    n, w = h_rows.shape
    n_exp = start.shape[0]
    chunk = min(chunk, n)
    return pl.pallas_call(
        functools.partial(_dispatch_kernel, n_tok=n, n_exp=n_exp),
        grid_spec=pltpu.PrefetchScalarGridSpec(
            num_scalar_prefetch=5,
            grid=(n // chunk,),
            in_specs=[pl.BlockSpec((chunk, w), lambda i, *_: (i, 0))],
            out_specs=pl.BlockSpec(memory_space=pl.ANY),
            scratch_shapes=[pltpu.VMEM((8, w), h_rows.dtype), pltpu.SemaphoreType.DMA(())],
        ),
        out_shape=jax.ShapeDtypeStruct((n_rows, w), h_rows.dtype),
        compiler_params=_params(("arbitrary",)),
        name="dispatch",
    )(e_flat, r_flat, start, fill_lo, fill_hi, h_rows)


def _expert_kernel(blk_e_ref, nused_ref, x_ref, wg_ref, wu_ref, wd_ref, o_ref):
    @pl.when(pl.program_id(0) < nused_ref[0])
    def _():
        xs = _unpack_bf16_pairs(x_ref[...])
        gate = jnp.dot(xs, wg_ref[0], preferred_element_type=F32)
        up = jnp.dot(xs, wu_ref[0], preferred_element_type=F32)
        hid = (jax.nn.silu(gate) * up).astype(BF16)
        o_ref[...] = jnp.dot(hid, wd_ref[0], preferred_element_type=F32)

    @pl.when(pl.program_id(0) >= nused_ref[0])
    def _():
        o_ref[...] = jnp.zeros_like(o_ref)


def _experts(buf, blk_e, n_used, w_gate, w_up, w_down):
    n_rows, half = buf.shape
    n_exp, d, f = w_gate.shape
    nb = n_rows // MOE_BLOCK

    def row_map(bi, blk_e_ref, nused_ref):
        return (jnp.minimum(bi, nused_ref[0] - 1), 0)

    def w_map(bi, blk_e_ref, nused_ref):
        return (blk_e_ref[bi], 0, 0)

    return pl.pallas_call(
        _expert_kernel,
        grid_spec=pltpu.PrefetchScalarGridSpec(
            num_scalar_prefetch=2,
            grid=(nb,),
            in_specs=[
                pl.BlockSpec((MOE_BLOCK, half), row_map),
                pl.BlockSpec((1, d, f), w_map, pipeline_mode=pl.Buffered(1)),
                pl.BlockSpec((1, d, f), w_map, pipeline_mode=pl.Buffered(1)),
                pl.BlockSpec((1, f, d), w_map, pipeline_mode=pl.Buffered(1)),
            ],
            out_specs=pl.BlockSpec((MOE_BLOCK, d), lambda bi, *_: (bi, 0)),
        ),
        out_shape=jax.ShapeDtypeStruct((n_rows, d), F32),
        compiler_params=_params(("arbitrary",)),
        name="experts",
    )(blk_e, n_used, buf, w_gate, w_up, w_down)


def _combine_kernel(e_ref, r_ref, start_ref, x_ref, g_ref, wcol_ref, ob_hbm, o_ref, gbuf, sem, *, nt, n_tok):
    tt = x_ref.shape[1]
    base = (pl.program_id(0) * nt + pl.program_id(1)) * tt

    def issue(t, _):
        for k in range(TOP_K):
            a = k * n_tok + base + t
            pltpu.make_async_copy(_row(ob_hbm, start_ref[e_ref[a]] + r_ref[a]), _row(gbuf.at[k], t), sem).start()
        return 0
    lax.fori_loop(0, tt, issue, 0)

    def drain(t, _):
        for k in range(TOP_K):
            pltpu.make_async_copy(_row(ob_hbm, 0), _row(gbuf.at[0], 0), sem).wait()
        return 0
    lax.fori_loop(0, tt, drain, 0)

    y = gbuf[0] * wcol_ref[:, 0:1] + gbuf[1] * wcol_ref[:, 1:2]
    o_ref[0] = x_ref[0] + g_ref[0] * y


def _combine(x, mod, g_chunk, wcol, out_buf, e_flat, r_flat, start, tt=256):
    b, s, d = x.shape
    tt = min(tt, s)
    nt = s // tt
    return pl.pallas_call(
        functools.partial(_combine_kernel, nt=nt, n_tok=b * s),
        grid_spec=pltpu.PrefetchScalarGridSpec(
            num_scalar_prefetch=3,
            grid=(b, nt),
            in_specs=[
                pl.BlockSpec((1, tt, d), lambda bi, i, *_: (bi, i, 0)),
                pl.BlockSpec((1, 1, d), lambda bi, i, *_: (bi, 0, g_chunk)),
                pl.BlockSpec((tt, LANES), lambda bi, i, *_: (bi * nt + i, 0)),
                pl.BlockSpec(memory_space=pl.ANY),
            ],
            out_specs=pl.BlockSpec((1, tt, d), lambda bi, i, *_: (bi, i, 0)),
            scratch_shapes=[pltpu.VMEM((TOP_K, tt, d), F32), pltpu.SemaphoreType.DMA(())],
        ),
        out_shape=jax.ShapeDtypeStruct((b, s, d), x.dtype),
        compiler_params=_params(("arbitrary", "arbitrary")),
        name="combine",
    )(e_flat, r_flat, start, x, mod, wcol, out_buf)


def _moe(x, g, mod, rw_pad, router_b, w_gate, w_up, w_down):
    b, s, d = x.shape
    n = b * s
    n_exp = router_b.shape[0]
    h_rows, meta, wcol, cnt = _router(x, g, mod, 4, 3, rw_pad, router_b)
    counts = cnt[:, 0]
    padded = (counts + MOE_BLOCK - 1) // MOE_BLOCK * MOE_BLOCK
    pad_end = jnp.cumsum(padded)
    start = pad_end - padded
    nb = (n * TOP_K + n_exp * (MOE_BLOCK - 1) + MOE_BLOCK - 1) // MOE_BLOCK
    blk_first = jnp.arange(nb, dtype=I32) * MOE_BLOCK
    blk_e = jnp.minimum(jnp.sum(blk_first[:, None] >= pad_end[None, :], axis=1), n_exp - 1).astype(I32)
    n_used = (pad_end[-1:] // MOE_BLOCK).astype(I32)
    e_flat = meta[0:2].reshape(-1)
    r_flat = meta[2:4].reshape(-1)
    n_rows = nb * MOE_BLOCK
    fill_lo = jnp.concatenate([start + counts, pad_end[-1:]]).astype(I32)
    fill_hi = jnp.concatenate([pad_end, jnp.full((1,), n_rows, pad_end.dtype)]).astype(I32)
    buf = _dispatch(h_rows.reshape(n, d // 2), e_flat, r_flat, start.astype(I32), fill_lo, fill_hi, n_rows)
    out_buf = _experts(buf, blk_e, n_used, w_gate, w_up, w_down)
    return _combine(x, mod, 5, wcol, out_buf, e_flat, r_flat, start.astype(I32))


def _pad_cols(w, width):
    return jnp.pad(w, ((0, 0), (0, width - w.shape[1])))


def kernel(x, c, ada_w, ada_b, norm1, norm2, w_in_ab, fox_fb, kv_norm, w_uk, w_uv, w_out_ab, w_in_c, conv_w, w_out_c, rel_bias, router_w, router_b, exp_gate, exp_up, exp_down, final_norm):
    depth, d = norm1.shape
    b, s, _ = x.shape
    fox_heads = fox_fb.shape[1]
    dsa_heads, rank = w_uk.shape[1], w_uk.shape[2]
    fox_w, dsa_w = fox_heads * HEAD_DIM, dsa_heads * HEAD_DIM
    even_cols = w_in_ab.shape[2]
    idx_heads = (even_cols - 3 * fox_w - fox_heads - dsa_w - rank - IDX_DIM) // (IDX_DIM + 1)
    iq_w = idx_heads * IDX_DIM
    o_fa = 3 * fox_w
    o_qb = o_fa + fox_heads
    o_ckv = o_qb + dsa_w
    o_iq = o_ckv + rank
    o_ik = o_iq + iq_w
    o_iw = o_ik + IDX_DIM
    assert o_iw + idx_heads == even_cols and fox_heads <= LANES and idx_heads <= LANES

    mods = _ada_mod(c, ada_w, ada_b)
    rw_pad = _pad_cols(router_w, LANES).T.astype(BF16)
    for l in range(depth):
        mod = mods[l].reshape(b, 1, 6 * d)
        i = l // 2
        h = _norm_mod(x, norm1[l], mod, 1, 0)
        if l % 2 == 0:
            w = w_in_ab[i]
            w_a = jnp.concatenate([w[:, :o_fa], w[:, o_qb:o_ckv], w[:, o_iq:o_ik]], axis=1).astype(BF16)
            w_s = jnp.concatenate([w[:, o_ckv:o_iq], _pad_cols(w[:, o_fa:o_qb], LANES),
                                   _pad_cols(w[:, o_ik:o_iw], LANES), _pad_cols(w[:, o_iw:], LANES)],
                                  axis=1).astype(BF16)
            z_a = _matmul(h, w_a, BF16)
            z_s = _matmul(h, w_s, F32)
            ckv_n, neg_cum, ik_e, ik_o, iw_s = _even_prep(z_s, kv_norm[i], fox_fb[i], idx_heads)
            o_a = _fox_attention(z_a, neg_cum, fox_heads)
            o_b = _dsa_attention(z_a, iw_s, ckv_n, ik_e, ik_o, w_uk[i].astype(BF16), w_uv[i].astype(BF16),
                                 rel_bias, fox_w, idx_heads)
            w_o = w_out_ab[i].astype(BF16)
            x = _matmul_residual([(o_a, w_o[:fox_w]), (o_b, w_o[fox_w:])], x, mod, 2)
        else:
            y = _shortconv(h, w_in_c[i].astype(BF16), conv_w[i])
            x = _matmul_residual([(y, w_out_c[i].astype(BF16))], x, mod, 2)
        x = _moe(x, norm2[l], mod, rw_pad, router_b, exp_gate[l].astype(BF16), exp_up[l].astype(BF16),
                 exp_down[l].astype(BF16))
    return _final_norm(x, final_norm)
```

```python
import functools
import math

import numpy as np
import jax
import jax.numpy as jnp
from jax import lax
from jax.experimental import pallas as pl
from jax.experimental.pallas import tpu as pltpu

F32 = jnp.float32
BF16 = jnp.bfloat16
I32 = jnp.int32
U32 = jnp.uint32

HEAD_DIM = 128
IDX_DIM = 64
TOPK_MAX = 256
N_BUCKETS = 32
MAX_DISTANCE = 128
N_GROUPS = 4
TOP_K = 2
EPS = 1e-6
LANES = 128
VMEM_LIMIT = 56 * 1024 * 1024
MOE_BLOCK = 256
DMA_ISSUE_UNROLL = 8
NT_DIMS = (((1,), (1,)), ((), ()))


def _params(sem, vmem=VMEM_LIMIT):
    return pltpu.CompilerParams(dimension_semantics=sem, vmem_limit_bytes=vmem)


def _ada_kernel(c_ref, w_ref, b_ref, o_ref):
    c = c_ref[...]
    ca = (c * jax.nn.sigmoid(c)).astype(BF16)
    o_ref[0] = jnp.dot(ca, w_ref[0].astype(BF16), preferred_element_type=F32) + b_ref[0]


def _ada_mod(c, ada_w, ada_b):
    depth, d, n6 = ada_w.shape
    b = c.shape[0]
    tn = min(512, n6)
    return pl.pallas_call(
        _ada_kernel,
        grid=(depth, n6 // tn),
        in_specs=[
            pl.BlockSpec((b, d), lambda l, j: (0, 0)),
            pl.BlockSpec((1, d, tn), lambda l, j: (l, 0, j)),
            pl.BlockSpec((1, 1, tn), lambda l, j: (l, 0, j)),
        ],
        out_specs=pl.BlockSpec((1, b, tn), lambda l, j: (l, 0, j)),
        out_shape=jax.ShapeDtypeStruct((depth, b, n6), F32),
        compiler_params=_params(("arbitrary", "arbitrary")),
        name="ada_mod",
    )(c, ada_w, ada_b.reshape(depth, 1, n6))


def _rms(x, g):
    return x * lax.rsqrt(jnp.mean(x * x, axis=-1, keepdims=True) + EPS) * g


def _norm_mod_kernel(x_ref, g_ref, sc_ref, sh_ref, o_ref):
    y = _rms(x_ref[0], g_ref[...])
    o_ref[0] = (y * (1.0 + sc_ref[0]) + sh_ref[0]).astype(o_ref.dtype)


def _mod_spec(d, chunk):
    return pl.BlockSpec((1, 1, d), lambda b, *_: (b, 0, chunk))


def _norm_mod(x, g, mod, sc_chunk, sh_chunk, ts=256):
    b, s, d = x.shape
    ts = min(ts, s)
    return pl.pallas_call(
        _norm_mod_kernel,
        grid=(b, s // ts),
        in_specs=[
            pl.BlockSpec((1, ts, d), lambda bi, i: (bi, i, 0)),
            pl.BlockSpec((1, d), lambda bi, i: (0, 0)),
            _mod_spec(d, sc_chunk),
            _mod_spec(d, sh_chunk),
        ],
        out_specs=pl.BlockSpec((1, ts, d), lambda bi, i: (bi, i, 0)),
        out_shape=jax.ShapeDtypeStruct((b, s, d), BF16),
        compiler_params=_params(("arbitrary", "arbitrary")),
        name="norm_mod",
    )(x, g.reshape(1, d), mod, mod)


def _mm_kernel(a_ref, w_ref, o_ref):
    o_ref[0] = jnp.dot(a_ref[0], w_ref[...], preferred_element_type=F32).astype(o_ref.dtype)


def _pick_tile(n, target):
    if n <= target:
        return n
    return max(t for t in range(LANES, target + 1, LANES) if n % t == 0)


def _matmul(a, w, out_dtype, tm=1024, tn=1024):
    b, s, k = a.shape
    n = w.shape[1]
    tm, tn = min(tm, s), _pick_tile(n, tn)
    return pl.pallas_call(
        _mm_kernel,
        grid=(b, s // tm, n // tn),
        in_specs=[
            pl.BlockSpec((1, tm, k), lambda bi, i, j: (bi, i, 0)),
            pl.BlockSpec((k, tn), lambda bi, i, j: (0, j)),
        ],
        out_specs=pl.BlockSpec((1, tm, tn), lambda bi, i, j: (bi, i, j)),
        out_shape=jax.ShapeDtypeStruct((b, s, n), out_dtype),
        compiler_params=_params(("arbitrary", "arbitrary", "arbitrary")),
        name="matmul",
    )(a, w)


def _mm_res_kernel(*refs, n_pairs):
    x_ref, g_ref, o_ref = refs[2 * n_pairs:]
    acc = None
    for p in range(n_pairs):
        part = jnp.dot(refs[2 * p][0], refs[2 * p + 1][...], preferred_element_type=F32)
        acc = part if acc is None else acc + part
    o_ref[0] = x_ref[0] + g_ref[0] * acc


def _matmul_residual(pairs, x, mod, g_chunk, tm=1024, tn=512):
    b, s, d = x.shape
    tm, tn = min(tm, s), min(tn, d)
    in_specs, args = [], []
    for a, w in pairs:
        k = a.shape[2]
        in_specs += [pl.BlockSpec((1, tm, k), lambda bi, i, j: (bi, i, 0)),
                     pl.BlockSpec((k, tn), lambda bi, i, j: (0, j))]
        args += [a, w]
    nj = d // tn
    in_specs += [pl.BlockSpec((1, tm, tn), lambda bi, i, j: (bi, i, j)),
                 pl.BlockSpec((1, 1, tn), lambda bi, i, j: (bi, 0, g_chunk * nj + j))]
    return pl.pallas_call(
        functools.partial(_mm_res_kernel, n_pairs=len(pairs)),
        grid=(b, s // tm, nj),
        in_specs=in_specs,
        out_specs=pl.BlockSpec((1, tm, tn), lambda bi, i, j: (bi, i, j)),
        out_shape=jax.ShapeDtypeStruct((b, s, d), x.dtype),
        compiler_params=_params(("arbitrary", "arbitrary", "arbitrary")),
        name="matmul_residual",
    )(*args, x, mod)


def _lane_cumsum(x):
    n = x.shape[-1]
    lane = lax.broadcasted_iota(I32, x.shape, x.ndim - 1)
    d = 1
    while d < n:
        x = x + jnp.where(lane >= d, pltpu.roll(x, d, x.ndim - 1), 0.0)
        d *= 2
    return x


def _even_prep_kernel(z_ref, kvn_ref, fb_ref, ckv_ref, negcum_ref, ike_ref, iko_ref, iw_ref,
                      *, rank, fox_heads, idx_heads):
    z = z_ref[0]
    ckv_ref[0] = _rms(z[:, :rank], kvn_ref[...]).astype(BF16)
    f_t = z[:, rank:rank + LANES].T
    logf = jax.nn.log_sigmoid(f_t[:fox_heads] + fb_ref[...])
    negcum_ref[0] = -_lane_cumsum(logf)
    ik = z[:, rank + LANES:rank + LANES + IDX_DIM].astype(BF16)
    zero = jnp.zeros_like(ik)
    ike_ref[0] = jnp.concatenate([ik, zero], axis=1)
    iko_ref[0] = jnp.concatenate([zero, ik], axis=1)
    iw_ref[0] = z[:, rank + 2 * LANES:rank + 2 * LANES + idx_heads] * (idx_heads ** -0.5) * (IDX_DIM ** -0.5)


def _even_prep(z_s, kv_norm, fox_fb, idx_heads):
    b, s, w = z_s.shape
    rank = kv_norm.shape[0]
    fh = fox_fb.shape[0]
    outs = (
        jax.ShapeDtypeStruct((b, s, rank), BF16),
        jax.ShapeDtypeStruct((b, fh, s), F32),
        jax.ShapeDtypeStruct((b, s, LANES), BF16),
        jax.ShapeDtypeStruct((b, s, LANES), BF16),
        jax.ShapeDtypeStruct((b, s, idx_heads), F32),
    )
    return pl.pallas_call(
        functools.partial(_even_prep_kernel, rank=rank, fox_heads=fh, idx_heads=idx_heads),
        grid=(b,),
        in_specs=[
            pl.BlockSpec((1, s, w), lambda bi: (bi, 0, 0)),
            pl.BlockSpec((1, rank), lambda bi: (0, 0)),
            pl.BlockSpec((fh, 1), lambda bi: (0, 0)),
        ],
        out_specs=(
            pl.BlockSpec((1, s, rank), lambda bi: (bi, 0, 0)),
            pl.BlockSpec((1, fh, s), lambda bi: (bi, 0, 0)),
            pl.BlockSpec((1, s, LANES), lambda bi: (bi, 0, 0)),
            pl.BlockSpec((1, s, LANES), lambda bi: (bi, 0, 0)),
            pl.BlockSpec((1, s, idx_heads), lambda bi: (bi, 0, 0)),
        ),
        out_shape=outs,
        compiler_params=_params(("arbitrary",)),
        name="even_prep",
    )(z_s, kv_norm.reshape(1, rank), fox_fb.reshape(fh, 1))


def _fox_kernel(q_ref, k_ref, v_ref, nc_ref, o_ref, *, heads, tq):
    s_len = q_ref.shape[1]
    scale = HEAD_DIM ** -0.5
    lower = lax.broadcasted_iota(I32, (tq, tq), 0) >= lax.broadcasted_iota(I32, (tq, tq), 1)
    for hh in range(heads):
        hs = slice(hh * HEAD_DIM, (hh + 1) * HEAD_DIM)

        def q_body(qi, _, hh=hh, hs=hs):
            q0 = pl.multiple_of(qi * tq, tq)
            q = q_ref[0, pl.ds(q0, tq), hs]

            def k_step(kj, carry, diagonal):
                m, l, acc = carry
                k0 = pl.multiple_of(kj * tq, tq)
                k = k_ref[0, pl.ds(k0, tq), hs]
                v = v_ref[0, pl.ds(k0, tq), hs]
                s = lax.dot_general(q, k, NT_DIMS, preferred_element_type=F32) * scale
                s = s + nc_ref[0, hh, pl.ds(kj, 1), :]
                if diagonal:
                    s = jnp.where(lower, s, -jnp.inf)
                m_new = jnp.maximum(m, jnp.max(s, axis=-1, keepdims=True))
                alpha = jnp.exp(m - m_new)
                p = jnp.exp(s - m_new)
                l = alpha * l + jnp.sum(p, axis=-1, keepdims=True)
                acc = alpha * acc + jnp.dot(p.astype(BF16), v, preferred_element_type=F32)
                return m_new, l, acc

            init = (jnp.full((tq, 1), -jnp.inf, F32), jnp.zeros((tq, 1), F32),
                    jnp.zeros((tq, HEAD_DIM), F32))
            carry = lax.fori_loop(0, qi, functools.partial(k_step, diagonal=False), init)
            m, l, acc = k_step(qi, carry, diagonal=True)
            o_ref[0, pl.ds(q0, tq), hs] = (acc / l).astype(o_ref.dtype)
            return 0

        lax.fori_loop(0, s_len // tq, q_body, 0)


def _fox_attention(z_a, neg_cum, fox_heads, heads_per_step=2, tq=512):
    b, s, _ = z_a.shape
    tq = tk = min(tq, s)
    hp = heads_per_step
    wblk = hp * HEAD_DIM
    nblk = fox_heads // hp
    nc = neg_cum.reshape(b, fox_heads, s // tk, tk)
    return pl.pallas_call(
        functools.partial(_fox_kernel, heads=hp, tq=tq),
        grid=(b, nblk),
        in_specs=[
            pl.BlockSpec((1, s, wblk), lambda bi, h: (bi, 0, h)),
            pl.BlockSpec((1, s, wblk), lambda bi, h: (bi, 0, nblk + h)),
            pl.BlockSpec((1, s, wblk), lambda bi, h: (bi, 0, 2 * nblk + h)),
            pl.BlockSpec((1, hp, s // tk, tk), lambda bi, h: (bi, h, 0, 0)),
        ],
        out_specs=pl.BlockSpec((1, s, wblk), lambda bi, h: (bi, 0, h)),
        out_shape=jax.ShapeDtypeStruct((b, s, fox_heads * HEAD_DIM), BF16),
        compiler_params=_params(("arbitrary", "arbitrary")),
        name="fox_attention",
    )(z_a, z_a, z_a, nc)


DSA_TQ = 128
DSA_KEY_CLASSES = 8
INT_MIN = int(np.iinfo(np.int32).min)


def _t5_thresholds(max_d):
    max_exact = N_BUCKETS // 2
    d = np.arange(max_d)
    df = np.maximum(d, 1).astype(np.float32)
    large = max_exact + (np.log(df / np.float32(max_exact)) / np.float32(math.log(MAX_DISTANCE / max_exact))
                         * np.float32(N_BUCKETS - max_exact)).astype(np.int32)
    table = np.where(d < max_exact, d, np.minimum(large, N_BUCKETS - 1))
    assert np.all(np.diff(table) >= 0) and table[-1] == N_BUCKETS - 1
    return [int(np.argmax(table >= b)) for b in range(N_BUCKETS)]


def _bias_tile_kernel(rb_ref, o_ref, *, tq, thresholds):
    h = pl.program_id(0)
    a = lax.broadcasted_iota(I32, (tq, 2 * tq), 0)
    c = lax.broadcasted_iota(I32, (tq, 2 * tq), 1)
    d = tq + a - c
    val = jnp.full((tq, 2 * tq), rb_ref[0, h], F32)
    for b in range(1, N_BUCKETS):
        val = jnp.where(d >= thresholds[b], rb_ref[b, h], val)
    o_ref[0] = val


def _bias_tiles(rel_bias, tq):
    heads = rel_bias.shape[1]
    thresholds = _t5_thresholds(2 * tq)
    assert thresholds[N_BUCKETS - 1] <= tq + 1
    return pl.pallas_call(
        functools.partial(_bias_tile_kernel, tq=tq, thresholds=thresholds),
        grid=(heads,),
        in_specs=[pl.BlockSpec(memory_space=pltpu.SMEM)],
        out_specs=pl.BlockSpec((1, tq, 2 * tq), lambda h: (h, 0, 0)),
        out_shape=jax.ShapeDtypeStruct((heads, tq, 2 * tq), F32),
        compiler_params=_params(("arbitrary",)),
        name="bias_tiles",
    )(rel_bias)


def _dsa_kernel(rb_ref, qb_ref, iq_ref, iw_ref, ckv_ref, ike_ref, iko_ref, wuk_ref, wuv_ref, tile_ref,
                o_ref, bias_sc, key_sc, madd_sc, qlat_sc, p_sc, o_sc,
                *, heads, idx_heads, topk, tq, group, i0):
    i = i0 + pl.program_id(0)
    b = pl.program_id(1)
    s_len, rank = ckv_ref.shape[1], ckv_ref.shape[2]
    nkb = s_len // tq
    scale = HEAD_DIM ** -0.5

    @pl.when(b == 0)
    def _():
        def fill(h, _):
            far = jnp.full((tq, tq), rb_ref[N_BUCKETS - 1, h], F32)
            for jb in range(nkb):
                bias_sc[h, jb] = far
            bias_sc[h, i] = tile_ref[h, :, tq:]

            @pl.when(i > 0)
            def _():
                bias_sc[h, i - 1] = tile_ref[h, :, :tq]
            return 0
        lax.fori_loop(0, heads, fill, 0)

    iq = iq_ref[0]
    iw = iw_ref[0]
    ike = ike_ref[0]
    iko = iko_ref[0]
    nch = idx_heads * IDX_DIM // LANES
    score = jnp.zeros((tq, s_len), F32)
    for g0 in range(0, nch, 4):
        cs = list(range(g0, min(g0 + 4, nch)))
        a = jnp.concatenate([iq[:, c * LANES:(c + 1) * LANES] for c in cs], axis=0)
        r_e = lax.dot_general(a, ike, NT_DIMS, preferred_element_type=F32)
        r_o = lax.dot_general(a, iko, NT_DIMS, preferred_element_type=F32)
        for n, c in enumerate(cs):
            rows = slice(n * tq, (n + 1) * tq)
            score = score + jnp.maximum(r_e[rows], 0.0) * iw[:, 2 * c:2 * c + 1]
            score = score + jnp.maximum(r_o[rows], 0.0) * iw[:, 2 * c + 1:2 * c + 2]

    t_idx = i * tq + lax.broadcasted_iota(I32, (tq, s_len), 0)
    s_idx = lax.broadcasted_iota(I32, (tq, s_len), 1)
    bits = pltpu.bitcast(score, I32)
    key = bits ^ ((bits >> 31) & 0x7FFFFFFF)
    key_sc[...] = jnp.where(t_idx >= s_idx, key, INT_MIN)

    def bit_step(it, lo):
        cand = lo + lax.shift_left(jnp.int32(1), 31 - it)
        cnt = jnp.sum(jnp.where(key_sc[...] >= cand, 1.0, 0.0), axis=-1, keepdims=True)
        return jnp.where(cnt >= topk, cand, lo)
    thr = lax.fori_loop(0, 32, bit_step, jnp.full((tq, 1), INT_MIN, I32))

    n_gt = jnp.sum(jnp.where(key_sc[...] > thr, 1.0, 0.0), axis=-1, keepdims=True)
    n_eq = jnp.sum(jnp.where(key_sc[...] == thr, 1.0, 0.0), axis=-1, keepdims=True)
    need = topk - n_gt
    nbits = (s_len - 1).bit_length()

    def idx_step(it, p):
        cand = p + lax.shift_left(jnp.int32(1), nbits - 1 - it)
        tied_below = jnp.where(key_sc[...] == thr, jnp.where(s_idx < cand, 1.0, 0.0), 0.0)
        cnt = jnp.sum(tied_below, axis=-1, keepdims=True)
        return jnp.where(cnt < need, cand, p)

    excess = jnp.where(jnp.logical_and(n_eq > need, thr != INT_MIN), 1, 0)
    p_last = lax.cond(jnp.max(excess) > 0,
                      lambda: lax.fori_loop(0, nbits, idx_step, jnp.zeros((tq, 1), I32)),
                      lambda: jnp.full((tq, 1), s_len, I32))

    key = key_sc[...]
    keep = jnp.where(key > thr, 0.0, jnp.where(key == thr, jnp.where(s_idx <= p_last, 0.0, -jnp.inf), -jnp.inf))
    madd_sc[...] = jnp.where(t_idx >= s_idx, keep, -jnp.inf)

    qb = qb_ref[0]
    for h in range(heads):
        qh = qb[:, h * HEAD_DIM:(h + 1) * HEAD_DIM]
        qlat_sc[h] = lax.dot_general(qh, wuk_ref[h], NT_DIMS, preferred_element_type=F32).astype(BF16)

    ckv = ckv_ref[0]

    def group_body(g, _):
        h0 = g * group
        qg = qlat_sc[pl.ds(h0, group)].reshape(group * tq, rank)
        sg = lax.dot_general(qg, ckv, NT_DIMS, preferred_element_type=F32) * scale
        madd = madd_sc[...]
        sums = []
        for hh in range(group):
            bias = jnp.concatenate([bias_sc[h0 + hh, jb] for jb in range(nkb)], axis=1)
            s = sg[hh * tq:(hh + 1) * tq] + bias + madd
            m = jnp.max(s, axis=-1, keepdims=True)
            p = jnp.exp(s - m)
            sums.append(jnp.sum(p, axis=-1, keepdims=True))
            p_sc[hh] = p.astype(BF16)
        og = jnp.dot(p_sc[...].reshape(group * tq, s_len), ckv, preferred_element_type=F32)
        for hh in range(group):
            ol = (og[hh * tq:(hh + 1) * tq] / sums[hh]).astype(BF16)
            o_sc[h0 + hh] = jnp.dot(ol, wuv_ref[h0 + hh], preferred_element_type=F32).astype(BF16)
        return 0
    lax.fori_loop(0, heads // group, group_body, 0)

    for h in range(heads):
        o_ref[0, :, h * HEAD_DIM:(h + 1) * HEAD_DIM] = o_sc[h]


def _dsa_attention(z_a, iw_s, ckv_n, ik_e, ik_o, w_uk, w_uv, rel_bias, fox_w, idx_heads, group=4):
    b, s, _ = z_a.shape
    heads, rank, _ = w_uk.shape
    tq = min(DSA_TQ, s)
    group = min(group, heads)
    assert s & (s - 1) == 0 and heads % group == 0
    dsa_w = heads * HEAD_DIM
    iq_w = idx_heads * IDX_DIM
    assert (3 * fox_w) % dsa_w == 0 and (3 * fox_w + dsa_w) % iq_w == 0
    qb_blk = 3 * fox_w // dsa_w
    iq_blk = (3 * fox_w + dsa_w) // iq_w
    topk = min(TOPK_MAX, s // 4)
    tiles = _bias_tiles(rel_bias, tq)
    nq = s // tq
    n_cls = min(DSA_KEY_CLASSES, nq)
    per = nq // n_cls
    assert nq % n_cls == 0
    outs = []
    for c in range(n_cls):
        i0 = c * per
        sk = (i0 + per) * tq
        outs.append(pl.pallas_call(
            functools.partial(_dsa_kernel, heads=heads, idx_heads=idx_heads, topk=topk, tq=tq, group=group,
                              i0=i0),
            grid=(per, b),
            in_specs=[
                pl.BlockSpec(memory_space=pltpu.SMEM),
                pl.BlockSpec((1, tq, dsa_w), lambda i, bi, i0=i0: (bi, i0 + i, qb_blk)),
                pl.BlockSpec((1, tq, iq_w), lambda i, bi, i0=i0: (bi, i0 + i, iq_blk)),
                pl.BlockSpec((1, tq, idx_heads), lambda i, bi, i0=i0: (bi, i0 + i, 0)),
                pl.BlockSpec((1, sk, rank), lambda i, bi: (bi, 0, 0)),
                pl.BlockSpec((1, sk, LANES), lambda i, bi: (bi, 0, 0)),
                pl.BlockSpec((1, sk, LANES), lambda i, bi: (bi, 0, 0)),
                pl.BlockSpec((heads, rank, HEAD_DIM), lambda i, bi: (0, 0, 0)),
                pl.BlockSpec((heads, rank, HEAD_DIM), lambda i, bi: (0, 0, 0)),
                pl.BlockSpec((heads, tq, 2 * tq), lambda i, bi: (0, 0, 0)),
            ],
            out_specs=pl.BlockSpec((1, tq, dsa_w), lambda i, bi: (bi, i, 0)),
            out_shape=jax.ShapeDtypeStruct((b, per * tq, dsa_w), BF16),
            scratch_shapes=[
                pltpu.VMEM((heads, sk // tq, tq, tq), F32),
                pltpu.VMEM((tq, sk), I32),
                pltpu.VMEM((tq, sk), F32),
                pltpu.VMEM((heads, tq, rank), BF16),
                pltpu.VMEM((group, tq, sk), BF16),
                pltpu.VMEM((heads, tq, HEAD_DIM), BF16),
            ],
            compiler_params=_params(("arbitrary", "arbitrary")),
            name=f"dsa_attention_{c}",
        )(rel_bias, z_a, z_a, iw_s, ckv_n, ik_e, ik_o, w_uk, w_uv, tiles))
    return jnp.concatenate(outs, axis=1)


def _shortconv_kernel(a_ref, wb_ref, wc_ref, wu_ref, cw_ref, o_ref):
    a = a_ref[0]
    bg = jnp.dot(a, wb_ref[...], preferred_element_type=F32)
    cg = jnp.dot(a, wc_ref[...], preferred_element_type=F32)
    u = jnp.dot(a, wu_ref[...], preferred_element_type=F32)
    p = cg * u
    row = lax.broadcasted_iota(I32, p.shape, 0)
    p1 = jnp.where(row >= 1, pltpu.roll(p, 1, 0), 0.0)
    p2 = jnp.where(row >= 2, pltpu.roll(p, 2, 0), 0.0)
    cw = cw_ref[...]
    y = p2 * cw[0:1] + p1 * cw[1:2] + p * cw[2:3]
    o_ref[0] = (bg * y).astype(o_ref.dtype)


def _shortconv(h, w_in, conv_w, tn=256):
    b, s, d = h.shape
    assert conv_w.shape[0] == 3
    tn = min(tn, d)
    nj = d // tn
    return pl.pallas_call(
        _shortconv_kernel,
        grid=(b, nj),
        in_specs=[
            pl.BlockSpec((1, s, d), lambda bi, j: (bi, 0, 0), pipeline_mode=pl.Buffered(1)),
            pl.BlockSpec((d, tn), lambda bi, j: (0, j)),
            pl.BlockSpec((d, tn), lambda bi, j: (0, nj + j)),
            pl.BlockSpec((d, tn), lambda bi, j: (0, 2 * nj + j)),
            pl.BlockSpec((3, tn), lambda bi, j: (0, j)),
        ],
        out_specs=pl.BlockSpec((1, s, tn), lambda bi, j: (bi, 0, j)),
        out_shape=jax.ShapeDtypeStruct((b, s, d), BF16),
        compiler_params=_params(("arbitrary", "arbitrary")),
        name="shortconv",
    )(h, w_in, w_in, w_in, conv_w)


def _pack_bf16_pairs(xb):
    half = xb.shape[1] // 2
    bits = pltpu.bitcast(xb.astype(F32), U32)
    return (bits[:, half:] & jnp.uint32(0xFFFF0000)) | (bits[:, :half] >> 16)


def _unpack_bf16_pairs(words):
    lo = pltpu.bitcast(words << 16, F32).astype(BF16)
    hi = pltpu.bitcast(words & jnp.uint32(0xFFFF0000), F32).astype(BF16)
    return jnp.concatenate([lo, hi], axis=1)


def _router_kernel(x_ref, g_ref, sc_ref, sh_ref, rw_ref, rb_ref, h_ref, meta_ref, wcol_ref, cnt_ref,
                   base_sc, *, n_exp):
    epg = n_exp // N_GROUPS
    first = jnp.logical_and(pl.program_id(0) == 0, pl.program_id(1) == 0)

    @pl.when(first)
    def _():
        base_sc[...] = jnp.zeros_like(base_sc)

    h = _rms(x_ref[0], g_ref[...]) * (1.0 + sc_ref[0]) + sh_ref[0]
    t, d = h.shape
    hb = h.astype(BF16)
    h_ref[0] = _pack_bf16_pairs(hb)

    logits_t = lax.dot_general(rw_ref[...], hb, NT_DIMS, preferred_element_type=F32)
    aff = jax.nn.sigmoid(logits_t[:n_exp])
    sel = aff + rb_ref[...]
    srow = [sel[e:e + 1] for e in range(n_exp)]
    arow = [aff[e:e + 1] for e in range(n_exp)]

    gscore = []
    for g in range(N_GROUPS):
        a0, a1, a2, a3 = srow[g * epg:(g + 1) * epg]
        hi1, lo1 = jnp.maximum(a0, a1), jnp.minimum(a0, a1)
        hi2, lo2 = jnp.maximum(a2, a3), jnp.minimum(a2, a3)
        gscore.append(jnp.maximum(hi1, hi2) + jnp.maximum(jnp.minimum(hi1, hi2), jnp.maximum(lo1, lo2)))
    grp = jnp.zeros((1, t), I32)
    best = gscore[0]
    for g in range(1, N_GROUPS):
        take = gscore[g] > best
        grp = jnp.where(take, g, grp)
        best = jnp.where(take, gscore[g], best)

    def pick(rows_, idx, n):
        out = rows_[0]
        for k in range(1, n):
            out = jnp.where(idx == k, rows_[k], out)
        return out

    in_s = [pick([srow[g * epg + k] for g in range(N_GROUPS)], grp, N_GROUPS) for k in range(epg)]
    in_a = [pick([arow[g * epg + k] for g in range(N_GROUPS)], grp, N_GROUPS) for k in range(epg)]

    i1 = jnp.zeros((1, t), I32)
    v1 = in_s[0]
    for k in range(1, epg):
        take = in_s[k] > v1
        i1 = jnp.where(take, k, i1)
        v1 = jnp.where(take, in_s[k], v1)
    i2 = jnp.where(i1 == 0, 1, 0)
    v2 = jnp.where(i1 == 0, in_s[1], in_s[0])
    for k in range(1, epg):
        take = jnp.logical_and(i1 != k, jnp.logical_and(i2 != k, in_s[k] > v2))
        i2 = jnp.where(take, k, i2)
        v2 = jnp.where(take, in_s[k], v2)
    e1 = grp * epg + i1
    e2 = grp * epg + i2
    a1 = pick(in_a, i1, epg)
    a2 = pick(in_a, i2, epg)
    denom = a1 + a2
    w1 = a1 / denom
    w2 = a2 / denom

    eiota = lax.broadcasted_iota(I32, (n_exp, t), 0)
    oh1 = jnp.where(eiota == e1, 1.0, 0.0)
    oh2 = jnp.where(eiota == e2, 1.0, 0.0)
    oh = oh1 + oh2
    before = lax.broadcasted_iota(I32, (t, t), 0) < lax.broadcasted_iota(I32, (t, t), 1)
    excl = jnp.dot(oh.astype(BF16), jnp.where(before, 1.0, 0.0).astype(BF16), preferred_element_type=F32)
    rank = base_sc[:, 0:1] + excl
    r1 = jnp.sum(oh1 * rank, axis=0, keepdims=True).astype(I32)
    r2 = jnp.sum(oh2 * rank, axis=0, keepdims=True).astype(I32)
    base_sc[...] = base_sc[...] + jnp.sum(oh, axis=1, keepdims=True)

    meta_ref[...] = jnp.concatenate([e1, e2, r1, r2, jnp.zeros((4, t), I32)], axis=0)
    wrows = jnp.concatenate([w1, w2, jnp.zeros((LANES - 2, t), F32)], axis=0)
    wcol_ref[...] = wrows.T
    cnt_ref[...] = base_sc[...].astype(I32)


def _router(x, g, mod, sc_chunk, sh_chunk, rw_pad, router_b, tt=256):
    b, s, d = x.shape
    n_exp = router_b.shape[0]
    assert n_exp // N_GROUPS == 4 and TOP_K == 2
    tt = min(tt, s)
    nt = s // tt
    n = b * s
    outs = (
        jax.ShapeDtypeStruct((b, s, d // 2), U32),
        jax.ShapeDtypeStruct((8, n), I32),
        jax.ShapeDtypeStruct((n, LANES), F32),
        jax.ShapeDtypeStruct((n_exp, LANES), I32),
    )
    return pl.pallas_call(
        functools.partial(_router_kernel, n_exp=n_exp),
        grid=(b, nt),
        in_specs=[
            pl.BlockSpec((1, tt, d), lambda bi, i: (bi, i, 0)),
            pl.BlockSpec((1, d), lambda bi, i: (0, 0)),
            _mod_spec(d, sc_chunk),
            _mod_spec(d, sh_chunk),
            pl.BlockSpec((LANES, d), lambda bi, i: (0, 0)),
            pl.BlockSpec((n_exp, 1), lambda bi, i: (0, 0)),
        ],
        out_specs=(
            pl.BlockSpec((1, tt, d // 2), lambda bi, i: (bi, i, 0)),
            pl.BlockSpec((8, tt), lambda bi, i: (0, bi * nt + i)),
            pl.BlockSpec((tt, LANES), lambda bi, i: (bi * nt + i, 0)),
            pl.BlockSpec((n_exp, LANES), lambda bi, i: (0, 0)),
        ),
        out_shape=outs,
        scratch_shapes=[pltpu.VMEM((n_exp, LANES), F32)],
        compiler_params=_params(("arbitrary", "arbitrary")),
        name="router",
    )(x, g.reshape(1, d), mod, mod, rw_pad, router_b.reshape(n_exp, 1))


def _row(ref, r):
    return ref.at[pl.ds(r, 1), :]


def _dispatch_kernel(e_ref, r_ref, start_ref, fill_lo_ref, fill_hi_ref, h_ref, buf_hbm, zero_sc, sem,
                     *, n_tok, n_exp):
    chunk = h_ref.shape[0]
    ci = pl.program_id(0)

    @pl.when(ci == 0)
    def _():
        zero_sc[...] = jnp.zeros_like(zero_sc)
        for e in range(n_exp + 1):
            lo, hi = fill_lo_ref[e], fill_hi_ref[e]

            def zfill(r, _):
                pltpu.make_async_copy(_row(zero_sc, 0), _row(buf_hbm, r), sem).start()
                return 0
            lax.fori_loop(lo, hi, zfill, 0)

            def zdrain(r, _):
                pltpu.make_async_copy(_row(zero_sc, 0), _row(buf_hbm, 0), sem).wait()
                return 0
            lax.fori_loop(lo, hi, zdrain, 0)

    def issue(t, _):
        for k in range(TOP_K):
            a = k * n_tok + ci * chunk + t
            pltpu.make_async_copy(_row(h_ref, t), _row(buf_hbm, start_ref[e_ref[a]] + r_ref[a]), sem).start()
        return 0
    lax.fori_loop(0, chunk, issue, 0, unroll=DMA_ISSUE_UNROLL)

    def drain(t, _):
        for k in range(TOP_K):
            pltpu.make_async_copy(_row(h_ref, 0), _row(buf_hbm, 0), sem).wait()
        return 0
    lax.fori_loop(0, chunk, drain, 0)


def _dispatch(h_rows, e_flat, r_flat, start, fill_lo, fill_hi, n_rows, chunk=256):
    n, w = h_rows.shape
    n_exp = start.shape[0]
    chunk = min(chunk, n)
    return pl.pallas_call(
        functools.partial(_dispatch_kernel, n_tok=n, n_exp=n_exp),
        grid_spec=pltpu.PrefetchScalarGridSpec(
            num_scalar_prefetch=5,
            grid=(n // chunk,),
            in_specs=[pl.BlockSpec((chunk, w), lambda i, *_: (i, 0))],
            out_specs=pl.BlockSpec(memory_space=pl.ANY),
            scratch_shapes=[pltpu.VMEM((8, w), h_rows.dtype), pltpu.SemaphoreType.DMA(())],
        ),
        out_shape=jax.ShapeDtypeStruct((n_rows, w), h_rows.dtype),
        compiler_params=_params(("arbitrary",)),
        name="dispatch",
    )(e_flat, r_flat, start, fill_lo, fill_hi, h_rows)


def _expert_kernel(blk_e_ref, nused_ref, x_ref, wg_ref, wu_ref, wd_ref, o_ref):
    @pl.when(pl.program_id(0) < nused_ref[0])
    def _():
        xs = _unpack_bf16_pairs(x_ref[...])
        gate = jnp.dot(xs, wg_ref[0], preferred_element_type=F32)
        up = jnp.dot(xs, wu_ref[0], preferred_element_type=F32)
        hid = (jax.nn.silu(gate) * up).astype(BF16)
        o_ref[...] = jnp.dot(hid, wd_ref[0], preferred_element_type=F32)

    @pl.when(pl.program_id(0) >= nused_ref[0])
    def _():
        o_ref[...] = jnp.zeros_like(o_ref)


def _experts(buf, blk_e, n_used, w_gate, w_up, w_down):
    n_rows, half = buf.shape
    n_exp, d, f = w_gate.shape
    nb = n_rows // MOE_BLOCK

    def row_map(bi, blk_e_ref, nused_ref):
        return (jnp.minimum(bi, nused_ref[0] - 1), 0)

    def w_map(bi, blk_e_ref, nused_ref):
        return (blk_e_ref[bi], 0, 0)

    return pl.pallas_call(
        _expert_kernel,
        grid_spec=pltpu.PrefetchScalarGridSpec(
            num_scalar_prefetch=2,
            grid=(nb,),
            in_specs=[
                pl.BlockSpec((MOE_BLOCK, half), row_map),
                pl.BlockSpec((1, d, f), w_map, pipeline_mode=pl.Buffered(1)),
                pl.BlockSpec((1, d, f), w_map, pipeline_mode=pl.Buffered(1)),
                pl.BlockSpec((1, f, d), w_map, pipeline_mode=pl.Buffered(1)),
            ],
            out_specs=pl.BlockSpec((MOE_BLOCK, d), lambda bi, *_: (bi, 0)),
        ),
        out_shape=jax.ShapeDtypeStruct((n_rows, d), F32),
        compiler_params=_params(("arbitrary",)),
        name="experts",
    )(blk_e, n_used, buf, w_gate, w_up, w_down)


def _combine_kernel(e_ref, r_ref, start_ref, x_ref, g_ref, wcol_ref, ob_hbm, ng_ref, *rest, nt, n_tok, last):
    if last:
        o_ref, gbuf, sem = rest
    else:
        sc_ref, sh_ref, o_ref, h_ref, gbuf, sem = rest
    tt = x_ref.shape[1]
    base = (pl.program_id(0) * nt + pl.program_id(1)) * tt

    def issue(t, _):
        for k in range(TOP_K):
            a = k * n_tok + base + t
            pltpu.make_async_copy(_row(ob_hbm, start_ref[e_ref[a]] + r_ref[a]), _row(gbuf.at[k], t), sem).start()
        return 0
    lax.fori_loop(0, tt, issue, 0, unroll=DMA_ISSUE_UNROLL)

    def drain(t, _):
        for k in range(TOP_K):
            pltpu.make_async_copy(_row(ob_hbm, 0), _row(gbuf.at[0], 0), sem).wait()
        return 0
    lax.fori_loop(0, tt, drain, 0)

    y = gbuf[0] * wcol_ref[:, 0:1] + gbuf[1] * wcol_ref[:, 1:2]
    x_new = x_ref[0] + g_ref[0] * y
    if last:
        o_ref[0] = _rms(x_new, ng_ref[...])
    else:
        o_ref[0] = x_new
        h_ref[0] = (_rms(x_new, ng_ref[...]) * (1.0 + sc_ref[0]) + sh_ref[0]).astype(h_ref.dtype)


def _combine(x, mod, g_chunk, wcol, out_buf, e_flat, r_flat, start, next_g, next_mod, tt=256):
    b, s, d = x.shape
    tt = min(tt, s)
    nt = s // tt
    last = next_mod is None
    row_spec = pl.BlockSpec((1, tt, d), lambda bi, i, *_: (bi, i, 0))
    in_specs = [
        row_spec,
        pl.BlockSpec((1, 1, d), lambda bi, i, *_: (bi, 0, g_chunk)),
        pl.BlockSpec((tt, LANES), lambda bi, i, *_: (bi * nt + i, 0)),
        pl.BlockSpec(memory_space=pl.ANY),
        pl.BlockSpec((1, d), lambda bi, i, *_: (0, 0)),
    ]
    args = [x, mod, wcol, out_buf, next_g.reshape(1, d)]
    if last:
        out_specs, out_shape = row_spec, jax.ShapeDtypeStruct((b, s, d), x.dtype)
    else:
        in_specs += [_mod_spec(d, 1), _mod_spec(d, 0)]
        args += [next_mod, next_mod]
        out_specs = (row_spec, row_spec)
        out_shape = (jax.ShapeDtypeStruct((b, s, d), x.dtype), jax.ShapeDtypeStruct((b, s, d), BF16))
    return pl.pallas_call(
        functools.partial(_combine_kernel, nt=nt, n_tok=b * s, last=last),
        grid_spec=pltpu.PrefetchScalarGridSpec(
            num_scalar_prefetch=3,
            grid=(b, nt),
            in_specs=in_specs,
            out_specs=out_specs,
            scratch_shapes=[pltpu.VMEM((TOP_K, tt, d), F32), pltpu.SemaphoreType.DMA(())],
        ),
        out_shape=out_shape,
        compiler_params=_params(("arbitrary", "arbitrary")),
        name="combine",
    )(e_flat, r_flat, start, *args)


def _moe(x, g, mod, rw_pad, router_b, w_gate, w_up, w_down, next_g, next_mod):
    b, s, d = x.shape
    n = b * s
    n_exp = router_b.shape[0]
    h_rows, meta, wcol, cnt = _router(x, g, mod, 4, 3, rw_pad, router_b)
    counts = cnt[:, 0]
    padded = (counts + MOE_BLOCK - 1) // MOE_BLOCK * MOE_BLOCK
    pad_end = jnp.cumsum(padded)
    start = pad_end - padded
    nb = (n * TOP_K + n_exp * (MOE_BLOCK - 1) + MOE_BLOCK - 1) // MOE_BLOCK
    blk_first = jnp.arange(nb, dtype=I32) * MOE_BLOCK
    blk_e = jnp.minimum(jnp.sum(blk_first[:, None] >= pad_end[None, :], axis=1), n_exp - 1).astype(I32)
    n_used = (pad_end[-1:] // MOE_BLOCK).astype(I32)
    e_flat = meta[0:2].reshape(-1)
    r_flat = meta[2:4].reshape(-1)
    n_rows = nb * MOE_BLOCK
    fill_lo = jnp.concatenate([start + counts, pad_end[-1:]]).astype(I32)
    fill_hi = jnp.concatenate([pad_end, jnp.full((1,), n_rows, pad_end.dtype)]).astype(I32)
    buf = _dispatch(h_rows.reshape(n, d // 2), e_flat, r_flat, start.astype(I32), fill_lo, fill_hi, n_rows)
    out_buf = _experts(buf, blk_e, n_used, w_gate, w_up, w_down)
    return _combine(x, mod, 5, wcol, out_buf, e_flat, r_flat, start.astype(I32), next_g, next_mod)


def _pad_cols(w, width):
    return jnp.pad(w, ((0, 0), (0, width - w.shape[1])))


def kernel(x, c, ada_w, ada_b, norm1, norm2, w_in_ab, fox_fb, kv_norm, w_uk, w_uv, w_out_ab, w_in_c, conv_w, w_out_c, rel_bias, router_w, router_b, exp_gate, exp_up, exp_down, final_norm):
    depth, d = norm1.shape
    b, s, _ = x.shape
    fox_heads = fox_fb.shape[1]
    dsa_heads, rank = w_uk.shape[1], w_uk.shape[2]
    fox_w, dsa_w = fox_heads * HEAD_DIM, dsa_heads * HEAD_DIM
    even_cols = w_in_ab.shape[2]
    idx_heads = (even_cols - 3 * fox_w - fox_heads - dsa_w - rank - IDX_DIM) // (IDX_DIM + 1)
    iq_w = idx_heads * IDX_DIM
    o_fa = 3 * fox_w
    o_qb = o_fa + fox_heads
    o_ckv = o_qb + dsa_w
    o_iq = o_ckv + rank
    o_ik = o_iq + iq_w
    o_iw = o_ik + IDX_DIM
    assert o_iw + idx_heads == even_cols and fox_heads <= LANES and idx_heads <= LANES

    mods = _ada_mod(c, ada_w, ada_b)
    rw_pad = _pad_cols(router_w, LANES).T.astype(BF16)
    mod_of = [mods[l].reshape(b, 1, 6 * d) for l in range(depth)]
    h = _norm_mod(x, norm1[0], mod_of[0], 1, 0)
    for l in range(depth):
        mod = mod_of[l]
        i = l // 2
        if l % 2 == 0:
            w = w_in_ab[i]
            w_a = jnp.concatenate([w[:, :o_fa], w[:, o_qb:o_ckv], w[:, o_iq:o_ik]], axis=1).astype(BF16)
            w_s = jnp.concatenate([w[:, o_ckv:o_iq], _pad_cols(w[:, o_fa:o_qb], LANES),
                                   _pad_cols(w[:, o_ik:o_iw], LANES), _pad_cols(w[:, o_iw:], LANES)],
                                  axis=1).astype(BF16)
            z_a = _matmul(h, w_a, BF16)
            z_s = _matmul(h, w_s, F32)
            ckv_n, neg_cum, ik_e, ik_o, iw_s = _even_prep(z_s, kv_norm[i], fox_fb[i], idx_heads)
            o_a = _fox_attention(z_a, neg_cum, fox_heads)
            o_b = _dsa_attention(z_a, iw_s, ckv_n, ik_e, ik_o, w_uk[i].astype(BF16), w_uv[i].astype(BF16),
                                 rel_bias, fox_w, idx_heads)
            w_o = w_out_ab[i].astype(BF16)
            x = _matmul_residual([(o_a, w_o[:fox_w]), (o_b, w_o[fox_w:])], x, mod, 2)
        else:
            y = _shortconv(h, w_in_c[i].astype(BF16), conv_w[i])
            x = _matmul_residual([(y, w_out_c[i].astype(BF16))], x, mod, 2)
        experts = (exp_gate[l].astype(BF16), exp_up[l].astype(BF16), exp_down[l].astype(BF16))
        if l + 1 < depth:
            x, h = _moe(x, norm2[l], mod, rw_pad, router_b, *experts, norm1[l + 1], mod_of[l + 1])
        else:
            x = _moe(x, norm2[l], mod, rw_pad, router_b, *experts, final_norm, None)
    return x
```

```python
import functools
import math

import numpy as np
import jax
import jax.numpy as jnp
from jax import lax
from jax.experimental import pallas as pl
from jax.experimental.pallas import tpu as pltpu

F32 = jnp.float32
BF16 = jnp.bfloat16
I32 = jnp.int32
U32 = jnp.uint32

HEAD_DIM = 128
IDX_DIM = 64
TOPK_MAX = 256
N_BUCKETS = 32
MAX_DISTANCE = 128
N_GROUPS = 4
TOP_K = 2
EPS = 1e-6
LANES = 128
VMEM_LIMIT = 56 * 1024 * 1024
MOE_BLOCK = 256
DMA_ISSUE_UNROLL = 8
NT_DIMS = (((1,), (1,)), ((), ()))


def _params(sem, vmem=VMEM_LIMIT):
    return pltpu.CompilerParams(dimension_semantics=sem, vmem_limit_bytes=vmem)


def _ada_kernel(c_ref, w_ref, b_ref, o_ref):
    c = c_ref[...]
    ca = (c * jax.nn.sigmoid(c)).astype(BF16)
    o_ref[0] = jnp.dot(ca, w_ref[0].astype(BF16), preferred_element_type=F32) + b_ref[0]


def _ada_mod(c, ada_w, ada_b):
    depth, d, n6 = ada_w.shape
    b = c.shape[0]
    tn = min(512, n6)
    return pl.pallas_call(
        _ada_kernel,
        grid=(depth, n6 // tn),
        in_specs=[
            pl.BlockSpec((b, d), lambda l, j: (0, 0)),
            pl.BlockSpec((1, d, tn), lambda l, j: (l, 0, j)),
            pl.BlockSpec((1, 1, tn), lambda l, j: (l, 0, j)),
        ],
        out_specs=pl.BlockSpec((1, b, tn), lambda l, j: (l, 0, j)),
        out_shape=jax.ShapeDtypeStruct((depth, b, n6), F32),
        compiler_params=_params(("arbitrary", "arbitrary")),
        name="ada_mod",
    )(c, ada_w, ada_b.reshape(depth, 1, n6))


def _rms(x, g):
    return x * lax.rsqrt(jnp.mean(x * x, axis=-1, keepdims=True) + EPS) * g


def _norm_mod_kernel(x_ref, g_ref, sc_ref, sh_ref, o_ref):
    y = _rms(x_ref[0], g_ref[...])
    o_ref[0] = (y * (1.0 + sc_ref[0]) + sh_ref[0]).astype(o_ref.dtype)


def _mod_spec(d, chunk):
    return pl.BlockSpec((1, 1, d), lambda b, *_: (b, 0, chunk))


def _norm_mod(x, g, mod, sc_chunk, sh_chunk, ts=256):
    b, s, d = x.shape
    ts = min(ts, s)
    return pl.pallas_call(
        _norm_mod_kernel,
        grid=(b, s // ts),
        in_specs=[
            pl.BlockSpec((1, ts, d), lambda bi, i: (bi, i, 0)),
            pl.BlockSpec((1, d), lambda bi, i: (0, 0)),
            _mod_spec(d, sc_chunk),
            _mod_spec(d, sh_chunk),
        ],
        out_specs=pl.BlockSpec((1, ts, d), lambda bi, i: (bi, i, 0)),
        out_shape=jax.ShapeDtypeStruct((b, s, d), BF16),
        compiler_params=_params(("arbitrary", "arbitrary")),
        name="norm_mod",
    )(x, g.reshape(1, d), mod, mod)


def _mm_kernel(a_ref, w_ref, o_ref):
    o_ref[0] = jnp.dot(a_ref[0], w_ref[...], preferred_element_type=F32).astype(o_ref.dtype)


def _pick_tile(n, target):
    if n <= target:
        return n
    return max(t for t in range(LANES, target + 1, LANES) if n % t == 0)


def _matmul(a, w, out_dtype, tm=1024, tn=1024):
    b, s, k = a.shape
    n = w.shape[1]
    tm, tn = min(tm, s), _pick_tile(n, tn)
    return pl.pallas_call(
        _mm_kernel,
        grid=(b, s // tm, n // tn),
        in_specs=[
            pl.BlockSpec((1, tm, k), lambda bi, i, j: (bi, i, 0)),
            pl.BlockSpec((k, tn), lambda bi, i, j: (0, j)),
        ],
        out_specs=pl.BlockSpec((1, tm, tn), lambda bi, i, j: (bi, i, j)),
        out_shape=jax.ShapeDtypeStruct((b, s, n), out_dtype),
        compiler_params=_params(("arbitrary", "arbitrary", "arbitrary")),
        name="matmul",
    )(a, w)


def _mm_res_kernel(*refs, n_pairs):
    x_ref, g_ref, o_ref = refs[2 * n_pairs:]
    acc = None
    for p in range(n_pairs):
        part = jnp.dot(refs[2 * p][0], refs[2 * p + 1][...], preferred_element_type=F32)
        acc = part if acc is None else acc + part
    o_ref[0] = x_ref[0] + g_ref[0] * acc


def _matmul_residual(pairs, x, mod, g_chunk, tm=1024, tn=512):
    b, s, d = x.shape
    tm, tn = min(tm, s), min(tn, d)
    in_specs, args = [], []
    for a, w in pairs:
        k = a.shape[2]
        in_specs += [pl.BlockSpec((1, tm, k), lambda bi, i, j: (bi, i, 0)),
                     pl.BlockSpec((k, tn), lambda bi, i, j: (0, j))]
        args += [a, w]
    nj = d // tn
    in_specs += [pl.BlockSpec((1, tm, tn), lambda bi, i, j: (bi, i, j)),
                 pl.BlockSpec((1, 1, tn), lambda bi, i, j: (bi, 0, g_chunk * nj + j))]
    return pl.pallas_call(
        functools.partial(_mm_res_kernel, n_pairs=len(pairs)),
        grid=(b, s // tm, nj),
        in_specs=in_specs,
        out_specs=pl.BlockSpec((1, tm, tn), lambda bi, i, j: (bi, i, j)),
        out_shape=jax.ShapeDtypeStruct((b, s, d), x.dtype),
        compiler_params=_params(("arbitrary", "arbitrary", "arbitrary")),
        name="matmul_residual",
    )(*args, x, mod)


def _lane_cumsum(x):
    n = x.shape[-1]
    lane = lax.broadcasted_iota(I32, x.shape, x.ndim - 1)
    d = 1
    while d < n:
        x = x + jnp.where(lane >= d, pltpu.roll(x, d, x.ndim - 1), 0.0)
        d *= 2
    return x


def _even_prep_kernel(z_ref, kvn_ref, fb_ref, ckv_ref, negcum_ref, ike_ref, iko_ref, iw_ref,
                      *, rank, fox_heads, idx_heads):
    z = z_ref[0]
    ckv_ref[0] = _rms(z[:, :rank], kvn_ref[...]).astype(BF16)
    f_t = z[:, rank:rank + LANES].T
    logf = jax.nn.log_sigmoid(f_t[:fox_heads] + fb_ref[...])
    negcum_ref[0] = -_lane_cumsum(logf)
    ik = z[:, rank + LANES:rank + LANES + IDX_DIM].astype(BF16)
    zero = jnp.zeros_like(ik)
    ike_ref[0] = jnp.concatenate([ik, zero], axis=1)
    iko_ref[0] = jnp.concatenate([zero, ik], axis=1)
    iw_ref[0] = z[:, rank + 2 * LANES:rank + 2 * LANES + idx_heads] * (idx_heads ** -0.5) * (IDX_DIM ** -0.5)


def _even_prep(z_s, kv_norm, fox_fb, idx_heads):
    b, s, w = z_s.shape
    rank = kv_norm.shape[0]
    fh = fox_fb.shape[0]
    outs = (
        jax.ShapeDtypeStruct((b, s, rank), BF16),
        jax.ShapeDtypeStruct((b, fh, s), F32),
        jax.ShapeDtypeStruct((b, s, LANES), BF16),
        jax.ShapeDtypeStruct((b, s, LANES), BF16),
        jax.ShapeDtypeStruct((b, s, idx_heads), F32),
    )
    return pl.pallas_call(
        functools.partial(_even_prep_kernel, rank=rank, fox_heads=fh, idx_heads=idx_heads),
        grid=(b,),
        in_specs=[
            pl.BlockSpec((1, s, w), lambda bi: (bi, 0, 0)),
            pl.BlockSpec((1, rank), lambda bi: (0, 0)),
            pl.BlockSpec((fh, 1), lambda bi: (0, 0)),
        ],
        out_specs=(
            pl.BlockSpec((1, s, rank), lambda bi: (bi, 0, 0)),
            pl.BlockSpec((1, fh, s), lambda bi: (bi, 0, 0)),
            pl.BlockSpec((1, s, LANES), lambda bi: (bi, 0, 0)),
            pl.BlockSpec((1, s, LANES), lambda bi: (bi, 0, 0)),
            pl.BlockSpec((1, s, idx_heads), lambda bi: (bi, 0, 0)),
        ),
        out_shape=outs,
        compiler_params=_params(("arbitrary",)),
        name="even_prep",
    )(z_s, kv_norm.reshape(1, rank), fox_fb.reshape(fh, 1))


def _fox_kernel(q_ref, k_ref, v_ref, nc_ref, o_ref, *, heads, tq):
    s_len = q_ref.shape[1]
    scale = HEAD_DIM ** -0.5
    lower = lax.broadcasted_iota(I32, (tq, tq), 0) >= lax.broadcasted_iota(I32, (tq, tq), 1)
    for hh in range(heads):
        hs = slice(hh * HEAD_DIM, (hh + 1) * HEAD_DIM)

        def q_body(qi, _, hh=hh, hs=hs):
            q0 = pl.multiple_of(qi * tq, tq)
            q = q_ref[0, pl.ds(q0, tq), hs]

            def k_step(kj, carry, diagonal):
                m, l, acc = carry
                k0 = pl.multiple_of(kj * tq, tq)
                k = k_ref[0, pl.ds(k0, tq), hs]
                v = v_ref[0, pl.ds(k0, tq), hs]
                s = lax.dot_general(q, k, NT_DIMS, preferred_element_type=F32) * scale
                s = s + nc_ref[0, hh, pl.ds(kj, 1), :]
                if diagonal:
                    s = jnp.where(lower, s, -jnp.inf)
                m_new = jnp.maximum(m, jnp.max(s, axis=-1, keepdims=True))
                alpha = jnp.exp(m - m_new)
                p = jnp.exp(s - m_new)
                l = alpha * l + jnp.sum(p, axis=-1, keepdims=True)
                acc = alpha * acc + jnp.dot(p.astype(BF16), v, preferred_element_type=F32)
                return m_new, l, acc

            init = (jnp.full((tq, 1), -jnp.inf, F32), jnp.zeros((tq, 1), F32),
                    jnp.zeros((tq, HEAD_DIM), F32))
            carry = lax.fori_loop(0, qi, functools.partial(k_step, diagonal=False), init)
            m, l, acc = k_step(qi, carry, diagonal=True)
            o_ref[0, pl.ds(q0, tq), hs] = (acc / l).astype(o_ref.dtype)
            return 0

        lax.fori_loop(0, s_len // tq, q_body, 0)


def _fox_attention(z_a, neg_cum, fox_heads, heads_per_step=2, tq=512):
    b, s, _ = z_a.shape
    tq = tk = min(tq, s)
    hp = heads_per_step
    wblk = hp * HEAD_DIM
    nblk = fox_heads // hp
    nc = neg_cum.reshape(b, fox_heads, s // tk, tk)
    return pl.pallas_call(
        functools.partial(_fox_kernel, heads=hp, tq=tq),
        grid=(b, nblk),
        in_specs=[
            pl.BlockSpec((1, s, wblk), lambda bi, h: (bi, 0, h)),
            pl.BlockSpec((1, s, wblk), lambda bi, h: (bi, 0, nblk + h)),
            pl.BlockSpec((1, s, wblk), lambda bi, h: (bi, 0, 2 * nblk + h)),
            pl.BlockSpec((1, hp, s // tk, tk), lambda bi, h: (bi, h, 0, 0)),
        ],
        out_specs=pl.BlockSpec((1, s, wblk), lambda bi, h: (bi, 0, h)),
        out_shape=jax.ShapeDtypeStruct((b, s, fox_heads * HEAD_DIM), BF16),
        compiler_params=_params(("arbitrary", "arbitrary")),
        name="fox_attention",
    )(z_a, z_a, z_a, nc)


DSA_TQ = 256
INT_MIN = int(np.iinfo(np.int32).min)


def _t5_thresholds(max_d):
    max_exact = N_BUCKETS // 2
    d = np.arange(max_d)
    df = np.maximum(d, 1).astype(np.float32)
    large = max_exact + (np.log(df / np.float32(max_exact)) / np.float32(math.log(MAX_DISTANCE / max_exact))
                         * np.float32(N_BUCKETS - max_exact)).astype(np.int32)
    table = np.where(d < max_exact, d, np.minimum(large, N_BUCKETS - 1))
    assert np.all(np.diff(table) >= 0) and table[-1] == N_BUCKETS - 1
    return [int(np.argmax(table >= b)) for b in range(N_BUCKETS)]


def _bias_tile_kernel(rb_ref, o_ref, *, tq, thresholds):
    h = pl.program_id(0)
    a = lax.broadcasted_iota(I32, (tq, 2 * tq), 0)
    c = lax.broadcasted_iota(I32, (tq, 2 * tq), 1)
    d = tq + a - c
    val = jnp.full((tq, 2 * tq), rb_ref[0, h], F32)
    for b in range(1, N_BUCKETS):
        val = jnp.where(d >= thresholds[b], rb_ref[b, h], val)
    o_ref[0] = val


def _bias_tiles(rel_bias, tq):
    heads = rel_bias.shape[1]
    thresholds = _t5_thresholds(2 * tq)
    assert thresholds[N_BUCKETS - 1] <= tq + 1
    return pl.pallas_call(
        functools.partial(_bias_tile_kernel, tq=tq, thresholds=thresholds),
        grid=(heads,),
        in_specs=[pl.BlockSpec(memory_space=pltpu.SMEM)],
        out_specs=pl.BlockSpec((1, tq, 2 * tq), lambda h: (h, 0, 0)),
        out_shape=jax.ShapeDtypeStruct((heads, tq, 2 * tq), F32),
        compiler_params=_params(("arbitrary",)),
        name="bias_tiles",
    )(rel_bias)


def _dsa_kernel(rb_ref, qb_ref, iq_ref, iw_ref, ckv_ref, ike_ref, iko_ref, wuk_ref, wuv_ref, tile_ref,
                o_ref, key_sc, madd_sc, p_sc, *, heads, idx_heads, topk, tq, group, i):
    s_len, rank = ckv_ref.shape[1], ckv_ref.shape[2]
    scale = HEAD_DIM ** -0.5

    iq = iq_ref[0]
    iw = iw_ref[0]
    ike = ike_ref[0]
    iko = iko_ref[0]
    nch = idx_heads * IDX_DIM // LANES
    score = jnp.zeros((tq, s_len), F32)
    for g0 in range(0, nch, 2):
        cs = list(range(g0, min(g0 + 2, nch)))
        a = jnp.concatenate([iq[:, c * LANES:(c + 1) * LANES] for c in cs], axis=0)
        r_e = lax.dot_general(a, ike, NT_DIMS, preferred_element_type=F32)
        r_o = lax.dot_general(a, iko, NT_DIMS, preferred_element_type=F32)
        for n, c in enumerate(cs):
            rows = slice(n * tq, (n + 1) * tq)
            score = score + jnp.maximum(r_e[rows], 0.0) * iw[:, 2 * c:2 * c + 1]
            score = score + jnp.maximum(r_o[rows], 0.0) * iw[:, 2 * c + 1:2 * c + 2]

    t_idx = i * tq + lax.broadcasted_iota(I32, (tq, s_len), 0)
    s_idx = lax.broadcasted_iota(I32, (tq, s_len), 1)
    bits = pltpu.bitcast(score, I32)
    key = bits ^ ((bits >> 31) & 0x7FFFFFFF)
    key_sc[...] = jnp.where(t_idx >= s_idx, key, INT_MIN)

    def bit_step(it, lo):
        cand = lo + lax.shift_left(jnp.int32(1), 31 - it)
        cnt = jnp.sum(jnp.where(key_sc[...] >= cand, 1.0, 0.0), axis=-1, keepdims=True)
        return jnp.where(cnt >= topk, cand, lo)
    thr = lax.fori_loop(0, 32, bit_step, jnp.full((tq, 1), INT_MIN, I32))

    n_gt = jnp.sum(jnp.where(key_sc[...] > thr, 1.0, 0.0), axis=-1, keepdims=True)
    n_eq = jnp.sum(jnp.where(key_sc[...] == thr, 1.0, 0.0), axis=-1, keepdims=True)
    need = topk - n_gt
    nbits = (s_len - 1).bit_length()

    def idx_step(it, p):
        cand = p + lax.shift_left(jnp.int32(1), nbits - 1 - it)
        tied_below = jnp.where(key_sc[...] == thr, jnp.where(s_idx < cand, 1.0, 0.0), 0.0)
        cnt = jnp.sum(tied_below, axis=-1, keepdims=True)
        return jnp.where(cnt < need, cand, p)

    excess = jnp.where(jnp.logical_and(n_eq > need, thr != INT_MIN), 1, 0)
    p_last = lax.cond(jnp.max(excess) > 0,
                      lambda: lax.fori_loop(0, nbits, idx_step, jnp.zeros((tq, 1), I32)),
                      lambda: jnp.full((tq, 1), s_len, I32))

    key = key_sc[...]
    keep = jnp.where(key > thr, 0.0, jnp.where(key == thr, jnp.where(s_idx <= p_last, 0.0, -jnp.inf), -jnp.inf))
    madd_sc[...] = jnp.where(t_idx >= s_idx, keep, -jnp.inf)

    qb = qb_ref[0]
    ckv = ckv_ref[0]
    madd = madd_sc[...]
    n_far = max(s_len - 2 * tq, 0)
    for g in range(heads // group):
        hs = range(g * group, (g + 1) * group)
        qg = jnp.concatenate(
            [lax.dot_general(qb[:, h * HEAD_DIM:(h + 1) * HEAD_DIM], wuk_ref[h], NT_DIMS,
                             preferred_element_type=F32).astype(BF16) for h in hs], axis=0)
        sg = lax.dot_general(qg, ckv, NT_DIMS, preferred_element_type=F32) * scale
        sums = []
        for n, h in enumerate(hs):
            near = tile_ref[h][:, 2 * tq - (s_len - n_far):]
            if n_far:
                far = jnp.full((tq, n_far), rb_ref[N_BUCKETS - 1, h], F32)
                near = jnp.concatenate([far, near], axis=1)
            s = sg[n * tq:(n + 1) * tq] + near + madd
            m = jnp.max(s, axis=-1, keepdims=True)
            p = jnp.exp(s - m)
            sums.append(jnp.sum(p, axis=-1, keepdims=True))
            p_sc[n] = p.astype(BF16)
        og = jnp.dot(p_sc[...].reshape(group * tq, s_len), ckv, preferred_element_type=F32)
        for n, h in enumerate(hs):
            ol = (og[n * tq:(n + 1) * tq] / sums[n]).astype(BF16)
            o_ref[0, :, h * HEAD_DIM:(h + 1) * HEAD_DIM] = jnp.dot(
                ol, wuv_ref[h], preferred_element_type=F32).astype(o_ref.dtype)


def _dsa_attention(z_a, iw_s, ckv_n, ik_e, ik_o, w_uk, w_uv, rel_bias, fox_w, idx_heads, group=2):
    b, s, _ = z_a.shape
    heads, rank, _ = w_uk.shape
    tq = min(DSA_TQ, s)
    group = min(group, heads)
    assert s & (s - 1) == 0 and heads % group == 0
    dsa_w = heads * HEAD_DIM
    iq_w = idx_heads * IDX_DIM
    assert (3 * fox_w) % dsa_w == 0 and (3 * fox_w + dsa_w) % iq_w == 0
    qb_blk = 3 * fox_w // dsa_w
    iq_blk = (3 * fox_w + dsa_w) // iq_w
    topk = min(TOPK_MAX, s // 4)
    tiles = _bias_tiles(rel_bias, tq)
    outs = []
    for i in range(s // tq):
        sk = (i + 1) * tq
        const = dict(pipeline_mode=pl.Buffered(1))
        outs.append(pl.pallas_call(
            functools.partial(_dsa_kernel, heads=heads, idx_heads=idx_heads, topk=topk, tq=tq, group=group, i=i),
            grid=(b,),
            in_specs=[
                pl.BlockSpec(memory_space=pltpu.SMEM),
                pl.BlockSpec((1, tq, dsa_w), lambda bi, i=i: (bi, i, qb_blk)),
                pl.BlockSpec((1, tq, iq_w), lambda bi, i=i: (bi, i, iq_blk)),
                pl.BlockSpec((1, tq, idx_heads), lambda bi, i=i: (bi, i, 0)),
                pl.BlockSpec((1, sk, rank), lambda bi: (bi, 0, 0)),
                pl.BlockSpec((1, sk, LANES), lambda bi: (bi, 0, 0)),
                pl.BlockSpec((1, sk, LANES), lambda bi: (bi, 0, 0)),
                pl.BlockSpec((heads, rank, HEAD_DIM), lambda bi: (0, 0, 0), **const),
                pl.BlockSpec((heads, rank, HEAD_DIM), lambda bi: (0, 0, 0), **const),
                pl.BlockSpec((heads, tq, 2 * tq), lambda bi: (0, 0, 0), **const),
            ],
            out_specs=pl.BlockSpec((1, tq, dsa_w), lambda bi: (bi, 0, 0)),
            out_shape=jax.ShapeDtypeStruct((b, tq, dsa_w), BF16),
            scratch_shapes=[
                pltpu.VMEM((tq, sk), I32),
                pltpu.VMEM((tq, sk), F32),
                pltpu.VMEM((group, tq, sk), BF16),
            ],
            compiler_params=_params(("arbitrary",)),
            name=f"dsa_attention_{i}",
        )(rel_bias, z_a, z_a, iw_s, ckv_n, ik_e, ik_o, w_uk, w_uv, tiles))
    return jnp.concatenate(outs, axis=1)


def _shortconv_kernel(a_ref, wb_ref, wc_ref, wu_ref, cw_ref, o_ref):
    a = a_ref[0]
    bg = jnp.dot(a, wb_ref[...], preferred_element_type=F32)
    cg = jnp.dot(a, wc_ref[...], preferred_element_type=F32)
    u = jnp.dot(a, wu_ref[...], preferred_element_type=F32)
    p = cg * u
    row = lax.broadcasted_iota(I32, p.shape, 0)
    p1 = jnp.where(row >= 1, pltpu.roll(p, 1, 0), 0.0)
    p2 = jnp.where(row >= 2, pltpu.roll(p, 2, 0), 0.0)
    cw = cw_ref[...]
    y = p2 * cw[0:1] + p1 * cw[1:2] + p * cw[2:3]
    o_ref[0] = (bg * y).astype(o_ref.dtype)


def _shortconv(h, w_in, conv_w, tn=256):
    b, s, d = h.shape
    assert conv_w.shape[0] == 3
    tn = min(tn, d)
    nj = d // tn
    return pl.pallas_call(
        _shortconv_kernel,
        grid=(b, nj),
        in_specs=[
            pl.BlockSpec((1, s, d), lambda bi, j: (bi, 0, 0), pipeline_mode=pl.Buffered(1)),
            pl.BlockSpec((d, tn), lambda bi, j: (0, j)),
            pl.BlockSpec((d, tn), lambda bi, j: (0, nj + j)),
            pl.BlockSpec((d, tn), lambda bi, j: (0, 2 * nj + j)),
            pl.BlockSpec((3, tn), lambda bi, j: (0, j)),
        ],
        out_specs=pl.BlockSpec((1, s, tn), lambda bi, j: (bi, 0, j)),
        out_shape=jax.ShapeDtypeStruct((b, s, d), BF16),
        compiler_params=_params(("arbitrary", "arbitrary")),
        name="shortconv",
    )(h, w_in, w_in, w_in, conv_w)


def _pack_bf16_pairs(xb):
    half = xb.shape[1] // 2
    bits = pltpu.bitcast(xb.astype(F32), U32)
    return (bits[:, half:] & jnp.uint32(0xFFFF0000)) | (bits[:, :half] >> 16)


def _unpack_bf16_pairs(words):
    lo = pltpu.bitcast(words << 16, F32).astype(BF16)
    hi = pltpu.bitcast(words & jnp.uint32(0xFFFF0000), F32).astype(BF16)
    return jnp.concatenate([lo, hi], axis=1)


def _router_kernel(x_ref, g_ref, sc_ref, sh_ref, rw_ref, rb_ref, h_ref, meta_ref, wcol_ref, cnt_ref,
                   base_sc, *, n_exp):
    epg = n_exp // N_GROUPS
    first = jnp.logical_and(pl.program_id(0) == 0, pl.program_id(1) == 0)

    @pl.when(first)
    def _():
        base_sc[...] = jnp.zeros_like(base_sc)

    h = _rms(x_ref[0], g_ref[...]) * (1.0 + sc_ref[0]) + sh_ref[0]
    t, d = h.shape
    hb = h.astype(BF16)
    h_ref[0] = _pack_bf16_pairs(hb)

    logits_t = lax.dot_general(rw_ref[...], hb, NT_DIMS, preferred_element_type=F32)
    aff = jax.nn.sigmoid(logits_t[:n_exp])
    sel = aff + rb_ref[...]
    srow = [sel[e:e + 1] for e in range(n_exp)]
    arow = [aff[e:e + 1] for e in range(n_exp)]

    gscore = []
    for g in range(N_GROUPS):
        a0, a1, a2, a3 = srow[g * epg:(g + 1) * epg]
        hi1, lo1 = jnp.maximum(a0, a1), jnp.minimum(a0, a1)
        hi2, lo2 = jnp.maximum(a2, a3), jnp.minimum(a2, a3)
        gscore.append(jnp.maximum(hi1, hi2) + jnp.maximum(jnp.minimum(hi1, hi2), jnp.maximum(lo1, lo2)))
    grp = jnp.zeros((1, t), I32)
    best = gscore[0]
    for g in range(1, N_GROUPS):
        take = gscore[g] > best
        grp = jnp.where(take, g, grp)
        best = jnp.where(take, gscore[g], best)

    def pick(rows_, idx, n):
        out = rows_[0]
        for k in range(1, n):
            out = jnp.where(idx == k, rows_[k], out)
        return out

    in_s = [pick([srow[g * epg + k] for g in range(N_GROUPS)], grp, N_GROUPS) for k in range(epg)]
    in_a = [pick([arow[g * epg + k] for g in range(N_GROUPS)], grp, N_GROUPS) for k in range(epg)]

    i1 = jnp.zeros((1, t), I32)
    v1 = in_s[0]
    for k in range(1, epg):
        take = in_s[k] > v1
        i1 = jnp.where(take, k, i1)
        v1 = jnp.where(take, in_s[k], v1)
    i2 = jnp.where(i1 == 0, 1, 0)
    v2 = jnp.where(i1 == 0, in_s[1], in_s[0])
    for k in range(1, epg):
        take = jnp.logical_and(i1 != k, jnp.logical_and(i2 != k, in_s[k] > v2))
        i2 = jnp.where(take, k, i2)
        v2 = jnp.where(take, in_s[k], v2)
    e1 = grp * epg + i1
    e2 = grp * epg + i2
    a1 = pick(in_a, i1, epg)
    a2 = pick(in_a, i2, epg)
    denom = a1 + a2
    w1 = a1 / denom
    w2 = a2 / denom

    eiota = lax.broadcasted_iota(I32, (n_exp, t), 0)
    oh1 = jnp.where(eiota == e1, 1.0, 0.0)
    oh2 = jnp.where(eiota == e2, 1.0, 0.0)
    oh = oh1 + oh2
    before = lax.broadcasted_iota(I32, (t, t), 0) < lax.broadcasted_iota(I32, (t, t), 1)
    excl = jnp.dot(oh.astype(BF16), jnp.where(before, 1.0, 0.0).astype(BF16), preferred_element_type=F32)
    rank = base_sc[:, 0:1] + excl
    r1 = jnp.sum(oh1 * rank, axis=0, keepdims=True).astype(I32)
    r2 = jnp.sum(oh2 * rank, axis=0, keepdims=True).astype(I32)
    base_sc[...] = base_sc[...] + jnp.sum(oh, axis=1, keepdims=True)

    meta_ref[...] = jnp.concatenate([e1, e2, r1, r2, jnp.zeros((4, t), I32)], axis=0)
    wrows = jnp.concatenate([w1, w2, jnp.zeros((LANES - 2, t), F32)], axis=0)
    wcol_ref[...] = wrows.T
    cnt_ref[...] = base_sc[...].astype(I32)


def _router(x, g, mod, sc_chunk, sh_chunk, rw_pad, router_b, tt=256):
    b, s, d = x.shape
    n_exp = router_b.shape[0]
    assert n_exp // N_GROUPS == 4 and TOP_K == 2
    tt = min(tt, s)
    nt = s // tt
    n = b * s
    outs = (
        jax.ShapeDtypeStruct((b, s, d // 2), U32),
        jax.ShapeDtypeStruct((8, n), I32),
        jax.ShapeDtypeStruct((n, LANES), F32),
        jax.ShapeDtypeStruct((n_exp, LANES), I32),
    )
    return pl.pallas_call(
        functools.partial(_router_kernel, n_exp=n_exp),
        grid=(b, nt),
        in_specs=[
            pl.BlockSpec((1, tt, d), lambda bi, i: (bi, i, 0)),
            pl.BlockSpec((1, d), lambda bi, i: (0, 0)),
            _mod_spec(d, sc_chunk),
            _mod_spec(d, sh_chunk),
            pl.BlockSpec((LANES, d), lambda bi, i: (0, 0)),
            pl.BlockSpec((n_exp, 1), lambda bi, i: (0, 0)),
        ],
        out_specs=(
            pl.BlockSpec((1, tt, d // 2), lambda bi, i: (bi, i, 0)),
            pl.BlockSpec((8, tt), lambda bi, i: (0, bi * nt + i)),
            pl.BlockSpec((tt, LANES), lambda bi, i: (bi * nt + i, 0)),
            pl.BlockSpec((n_exp, LANES), lambda bi, i: (0, 0)),
        ),
        out_shape=outs,
        scratch_shapes=[pltpu.VMEM((n_exp, LANES), F32)],
        compiler_params=_params(("arbitrary", "arbitrary")),
        name="router",
    )(x, g.reshape(1, d), mod, mod, rw_pad, router_b.reshape(n_exp, 1))


def _row(ref, r):
    return ref.at[pl.ds(r, 1), :]


def _dispatch_kernel(e_ref, r_ref, start_ref, fill_lo_ref, fill_hi_ref, h_ref, buf_hbm, zero_sc, sem,
                     *, n_tok, n_exp):
    chunk = h_ref.shape[0]
    ci = pl.program_id(0)

    @pl.when(ci == 0)
    def _():
        zero_sc[...] = jnp.zeros_like(zero_sc)
        for e in range(n_exp + 1):
            lo, hi = fill_lo_ref[e], fill_hi_ref[e]

            def zfill(r, _):
                pltpu.make_async_copy(_row(zero_sc, 0), _row(buf_hbm, r), sem).start()
                return 0
            lax.fori_loop(lo, hi, zfill, 0)

            def zdrain(r, _):
                pltpu.make_async_copy(_row(zero_sc, 0), _row(buf_hbm, 0), sem).wait()
                return 0
            lax.fori_loop(lo, hi, zdrain, 0)

    def issue(t, _):
        for k in range(TOP_K):
            a = k * n_tok + ci * chunk + t
            pltpu.make_async_copy(_row(h_ref, t), _row(buf_hbm, start_ref[e_ref[a]] + r_ref[a]), sem).start()
        return 0
    lax.fori_loop(0, chunk, issue, 0, unroll=DMA_ISSUE_UNROLL)

    def drain(t, _):
        for k in range(TOP_K):
            pltpu.make_async_copy(_row(h_ref, 0), _row(buf_hbm, 0), sem).wait()
        return 0
    lax.fori_loop(0, chunk, drain, 0)


def _dispatch(h_rows, e_flat, r_flat, start, fill_lo, fill_hi, n_rows, chunk=256):
    n, w = h_rows.shape
    n_exp = start.shape[0]
    chunk = min(chunk, n)
    return pl.pallas_call(
        functools.partial(_dispatch_kernel, n_tok=n, n_exp=n_exp),
        grid_spec=pltpu.PrefetchScalarGridSpec(
            num_scalar_prefetch=5,
            grid=(n // chunk,),
            in_specs=[pl.BlockSpec((chunk, w), lambda i, *_: (i, 0))],
            out_specs=pl.BlockSpec(memory_space=pl.ANY),
            scratch_shapes=[pltpu.VMEM((8, w), h_rows.dtype), pltpu.SemaphoreType.DMA(())],
        ),
        out_shape=jax.ShapeDtypeStruct((n_rows, w), h_rows.dtype),
        compiler_params=_params(("arbitrary",)),
        name="dispatch",
    )(e_flat, r_flat, start, fill_lo, fill_hi, h_rows)


def _expert_kernel(blk_e_ref, nused_ref, x_ref, wg_ref, wu_ref, wd_ref, o_ref):
    @pl.when(pl.program_id(0) < nused_ref[0])
    def _():
        xs = _unpack_bf16_pairs(x_ref[...])
        gate = jnp.dot(xs, wg_ref[0], preferred_element_type=F32)
        up = jnp.dot(xs, wu_ref[0], preferred_element_type=F32)
        hid = (jax.nn.silu(gate) * up).astype(BF16)
        o_ref[...] = jnp.dot(hid, wd_ref[0], preferred_element_type=F32)

    @pl.when(pl.program_id(0) >= nused_ref[0])
    def _():
        o_ref[...] = jnp.zeros_like(o_ref)


def _experts(buf, blk_e, n_used, layer, w_gate, w_up, w_down):
    n_rows, half = buf.shape
    _, n_exp, d, f = w_gate.shape
    nb = n_rows // MOE_BLOCK

    def row_map(bi, blk_e_ref, nused_ref):
        return (jnp.minimum(bi, nused_ref[0] - 1), 0)

    def w_map(bi, blk_e_ref, nused_ref):
        return (layer, blk_e_ref[bi], 0, 0)

    return pl.pallas_call(
        _expert_kernel,
        grid_spec=pltpu.PrefetchScalarGridSpec(
            num_scalar_prefetch=2,
            grid=(nb,),
            in_specs=[
                pl.BlockSpec((MOE_BLOCK, half), row_map),
                pl.BlockSpec((None, 1, d, f), w_map, pipeline_mode=pl.Buffered(1)),
                pl.BlockSpec((None, 1, d, f), w_map, pipeline_mode=pl.Buffered(1)),
                pl.BlockSpec((None, 1, f, d), w_map, pipeline_mode=pl.Buffered(1)),
            ],
            out_specs=pl.BlockSpec((MOE_BLOCK, d), lambda bi, *_: (bi, 0)),
        ),
        out_shape=jax.ShapeDtypeStruct((n_rows, d), F32),
        compiler_params=_params(("arbitrary",)),
        name="experts",
    )(blk_e, n_used, buf, w_gate, w_up, w_down)


def _combine_kernel(e_ref, r_ref, start_ref, x_ref, g_ref, wcol_ref, ob_hbm, ng_ref, *rest, nt, n_tok, last):
    if last:
        o_ref, gbuf, sem = rest
    else:
        sc_ref, sh_ref, o_ref, h_ref, gbuf, sem = rest
    tt = x_ref.shape[1]
    base = (pl.program_id(0) * nt + pl.program_id(1)) * tt

    def issue(t, _):
        for k in range(TOP_K):
            a = k * n_tok + base + t
            pltpu.make_async_copy(_row(ob_hbm, start_ref[e_ref[a]] + r_ref[a]), _row(gbuf.at[k], t), sem).start()
        return 0
    lax.fori_loop(0, tt, issue, 0, unroll=DMA_ISSUE_UNROLL)

    def drain(t, _):
        for k in range(TOP_K):
            pltpu.make_async_copy(_row(ob_hbm, 0), _row(gbuf.at[0], 0), sem).wait()
        return 0
    lax.fori_loop(0, tt, drain, 0)

    y = gbuf[0] * wcol_ref[:, 0:1] + gbuf[1] * wcol_ref[:, 1:2]
    x_new = x_ref[0] + g_ref[0] * y
    if last:
        o_ref[0] = _rms(x_new, ng_ref[...])
    else:
        o_ref[0] = x_new
        h_ref[0] = (_rms(x_new, ng_ref[...]) * (1.0 + sc_ref[0]) + sh_ref[0]).astype(h_ref.dtype)


def _combine(x, mod, g_chunk, wcol, out_buf, e_flat, r_flat, start, next_g, next_mod, tt=256):
    b, s, d = x.shape
    tt = min(tt, s)
    nt = s // tt
    last = next_mod is None
    row_spec = pl.BlockSpec((1, tt, d), lambda bi, i, *_: (bi, i, 0))
    in_specs = [
        row_spec,
        pl.BlockSpec((1, 1, d), lambda bi, i, *_: (bi, 0, g_chunk)),
        pl.BlockSpec((tt, LANES), lambda bi, i, *_: (bi * nt + i, 0)),
        pl.BlockSpec(memory_space=pl.ANY),
        pl.BlockSpec((1, d), lambda bi, i, *_: (0, 0)),
    ]
    args = [x, mod, wcol, out_buf, next_g.reshape(1, d)]
    if last:
        out_specs, out_shape = row_spec, jax.ShapeDtypeStruct((b, s, d), x.dtype)
    else:
        in_specs += [_mod_spec(d, 1), _mod_spec(d, 0)]
        args += [next_mod, next_mod]
        out_specs = (row_spec, row_spec)
        out_shape = (jax.ShapeDtypeStruct((b, s, d), x.dtype), jax.ShapeDtypeStruct((b, s, d), BF16))
    return pl.pallas_call(
        functools.partial(_combine_kernel, nt=nt, n_tok=b * s, last=last),
        grid_spec=pltpu.PrefetchScalarGridSpec(
            num_scalar_prefetch=3,
            grid=(b, nt),
            in_specs=in_specs,
            out_specs=out_specs,
            scratch_shapes=[pltpu.VMEM((TOP_K, tt, d), F32), pltpu.SemaphoreType.DMA(())],
        ),
        out_shape=out_shape,
        compiler_params=_params(("arbitrary", "arbitrary")),
        name="combine",
    )(e_flat, r_flat, start, *args)


def _moe(x, g, mod, rw_pad, router_b, layer, w_gate, w_up, w_down, next_g, next_mod):
    b, s, d = x.shape
    n = b * s
    n_exp = router_b.shape[0]
    h_rows, meta, wcol, cnt = _router(x, g, mod, 4, 3, rw_pad, router_b)
    counts = cnt[:, 0]
    padded = (counts + MOE_BLOCK - 1) // MOE_BLOCK * MOE_BLOCK
    pad_end = jnp.cumsum(padded)
    start = pad_end - padded
    nb = (n * TOP_K + n_exp * (MOE_BLOCK - 1) + MOE_BLOCK - 1) // MOE_BLOCK
    blk_first = jnp.arange(nb, dtype=I32) * MOE_BLOCK
    blk_e = jnp.minimum(jnp.sum(blk_first[:, None] >= pad_end[None, :], axis=1), n_exp - 1).astype(I32)
    n_used = (pad_end[-1:] // MOE_BLOCK).astype(I32)
    e_flat = meta[0:2].reshape(-1)
    r_flat = meta[2:4].reshape(-1)
    n_rows = nb * MOE_BLOCK
    fill_lo = jnp.concatenate([start + counts, pad_end[-1:]]).astype(I32)
    fill_hi = jnp.concatenate([pad_end, jnp.full((1,), n_rows, pad_end.dtype)]).astype(I32)
    buf = _dispatch(h_rows.reshape(n, d // 2), e_flat, r_flat, start.astype(I32), fill_lo, fill_hi, n_rows)
    out_buf = _experts(buf, blk_e, n_used, layer, w_gate, w_up, w_down)
    return _combine(x, mod, 5, wcol, out_buf, e_flat, r_flat, start.astype(I32), next_g, next_mod)


def _pad_cols(w, width):
    return jnp.pad(w, ((0, 0), (0, width - w.shape[1])))


def kernel(x, c, ada_w, ada_b, norm1, norm2, w_in_ab, fox_fb, kv_norm, w_uk, w_uv, w_out_ab, w_in_c, conv_w, w_out_c, rel_bias, router_w, router_b, exp_gate, exp_up, exp_down, final_norm):
    depth, d = norm1.shape
    b, s, _ = x.shape
    fox_heads = fox_fb.shape[1]
    dsa_heads, rank = w_uk.shape[1], w_uk.shape[2]
    fox_w, dsa_w = fox_heads * HEAD_DIM, dsa_heads * HEAD_DIM
    even_cols = w_in_ab.shape[2]
    idx_heads = (even_cols - 3 * fox_w - fox_heads - dsa_w - rank - IDX_DIM) // (IDX_DIM + 1)
    iq_w = idx_heads * IDX_DIM
    o_fa = 3 * fox_w
    o_qb = o_fa + fox_heads
    o_ckv = o_qb + dsa_w
    o_iq = o_ckv + rank
    o_ik = o_iq + iq_w
    o_iw = o_ik + IDX_DIM
    assert o_iw + idx_heads == even_cols and fox_heads <= LANES and idx_heads <= LANES

    mods = _ada_mod(c, ada_w, ada_b)
    rw_pad = _pad_cols(router_w, LANES).T.astype(BF16)
    experts = (exp_gate.astype(BF16), exp_up.astype(BF16), exp_down.astype(BF16))
    mod_of = [mods[l].reshape(b, 1, 6 * d) for l in range(depth)]
    h = _norm_mod(x, norm1[0], mod_of[0], 1, 0)
    for l in range(depth):
        mod = mod_of[l]
        i = l // 2
        if l % 2 == 0:
            w = w_in_ab[i]
            w_a = jnp.concatenate([w[:, :o_fa], w[:, o_qb:o_ckv], w[:, o_iq:o_ik]], axis=1).astype(BF16)
            w_s = jnp.concatenate([w[:, o_ckv:o_iq], _pad_cols(w[:, o_fa:o_qb], LANES),
                                   _pad_cols(w[:, o_ik:o_iw], LANES), _pad_cols(w[:, o_iw:], LANES)],
                                  axis=1).astype(BF16)
            z_a = _matmul(h, w_a, BF16)
            z_s = _matmul(h, w_s, F32)
            ckv_n, neg_cum, ik_e, ik_o, iw_s = _even_prep(z_s, kv_norm[i], fox_fb[i], idx_heads)
            o_a = _fox_attention(z_a, neg_cum, fox_heads)
            o_b = _dsa_attention(z_a, iw_s, ckv_n, ik_e, ik_o, w_uk[i].astype(BF16), w_uv[i].astype(BF16),
                                 rel_bias, fox_w, idx_heads)
            w_o = w_out_ab[i].astype(BF16)
            x = _matmul_residual([(o_a, w_o[:fox_w]), (o_b, w_o[fox_w:])], x, mod, 2)
        else:
            y = _shortconv(h, w_in_c[i].astype(BF16), conv_w[i])
            x = _matmul_residual([(y, w_out_c[i].astype(BF16))], x, mod, 2)
        if l + 1 < depth:
            x, h = _moe(x, norm2[l], mod, rw_pad, router_b, l, *experts, norm1[l + 1], mod_of[l + 1])
        else:
            x = _moe(x, norm2[l], mod, rw_pad, router_b, l, *experts, final_norm, None)
    return x
```

```python
import functools
import math

import numpy as np
import jax
import jax.numpy as jnp
from jax import lax
from jax.experimental import pallas as pl
from jax.experimental.pallas import tpu as pltpu

F32 = jnp.float32
BF16 = jnp.bfloat16
I32 = jnp.int32
U32 = jnp.uint32

HEAD_DIM = 128
IDX_DIM = 64
TOPK_MAX = 256
N_BUCKETS = 32
MAX_DISTANCE = 128
N_GROUPS = 4
TOP_K = 2
EPS = 1e-6
LANES = 128
VMEM_LIMIT = 56 * 1024 * 1024
MOE_BLOCK = 256
COMBINE_ROWS = 64
DMA_ISSUE_UNROLL = 8
NT_DIMS = (((1,), (1,)), ((), ()))


def _params(sem, vmem=VMEM_LIMIT):
    return pltpu.CompilerParams(dimension_semantics=sem, vmem_limit_bytes=vmem)


def _ada_kernel(c_ref, w_ref, b_ref, o_ref):
    c = c_ref[...]
    ca = (c * jax.nn.sigmoid(c)).astype(BF16)
    o_ref[0] = jnp.dot(ca, w_ref[0].astype(BF16), preferred_element_type=F32) + b_ref[0]


def _ada_mod(c, ada_w, ada_b):
    depth, d, n6 = ada_w.shape
    b = c.shape[0]
    tn = min(512, n6)
    return pl.pallas_call(
        _ada_kernel,
        grid=(depth, n6 // tn),
        in_specs=[
            pl.BlockSpec((b, d), lambda l, j: (0, 0)),
            pl.BlockSpec((1, d, tn), lambda l, j: (l, 0, j)),
            pl.BlockSpec((1, 1, tn), lambda l, j: (l, 0, j)),
        ],
        out_specs=pl.BlockSpec((1, b, tn), lambda l, j: (l, 0, j)),
        out_shape=jax.ShapeDtypeStruct((depth, b, n6), F32),
        compiler_params=_params(("arbitrary", "arbitrary")),
        name="ada_mod",
    )(c, ada_w, ada_b.reshape(depth, 1, n6))


def _rms(x, g):
    return x * lax.rsqrt(jnp.mean(x * x, axis=-1, keepdims=True) + EPS) * g


def _norm_mod_kernel(x_ref, g_ref, sc_ref, sh_ref, o_ref):
    y = _rms(x_ref[0], g_ref[...])
    o_ref[0] = (y * (1.0 + sc_ref[0]) + sh_ref[0]).astype(o_ref.dtype)


def _mod_spec(d, chunk):
    return pl.BlockSpec((1, 1, d), lambda b, *_: (b, 0, chunk))


def _norm_mod(x, g, mod, sc_chunk, sh_chunk, ts=256):
    b, s, d = x.shape
    ts = min(ts, s)
    return pl.pallas_call(
        _norm_mod_kernel,
        grid=(b, s // ts),
        in_specs=[
            pl.BlockSpec((1, ts, d), lambda bi, i: (bi, i, 0)),
            pl.BlockSpec((1, d), lambda bi, i: (0, 0)),
            _mod_spec(d, sc_chunk),
            _mod_spec(d, sh_chunk),
        ],
        out_specs=pl.BlockSpec((1, ts, d), lambda bi, i: (bi, i, 0)),
        out_shape=jax.ShapeDtypeStruct((b, s, d), BF16),
        compiler_params=_params(("arbitrary", "arbitrary")),
        name="norm_mod",
    )(x, g.reshape(1, d), mod, mod)


def _mm_kernel(a_ref, w_ref, o_ref):
    o_ref[0] = jnp.dot(a_ref[0], w_ref[...], preferred_element_type=F32).astype(o_ref.dtype)


def _pick_tile(n, target):
    if n <= target:
        return n
    return max(t for t in range(LANES, target + 1, LANES) if n % t == 0)


def _matmul(a, w, out_dtype, tm=1024, tn=1024):
    b, s, k = a.shape
    n = w.shape[1]
    tm, tn = min(tm, s), _pick_tile(n, tn)
    return pl.pallas_call(
        _mm_kernel,
        grid=(b, s // tm, n // tn),
        in_specs=[
            pl.BlockSpec((1, tm, k), lambda bi, i, j: (bi, i, 0)),
            pl.BlockSpec((k, tn), lambda bi, i, j: (0, j)),
        ],
        out_specs=pl.BlockSpec((1, tm, tn), lambda bi, i, j: (bi, i, j)),
        out_shape=jax.ShapeDtypeStruct((b, s, n), out_dtype),
        compiler_params=_params(("arbitrary", "arbitrary", "arbitrary")),
        name="matmul",
    )(a, w)


def _mm_res_kernel(*refs, n_pairs):
    x_ref, g_ref, o_ref = refs[2 * n_pairs:]
    acc = None
    for p in range(n_pairs):
        part = jnp.dot(refs[2 * p][0], refs[2 * p + 1][...], preferred_element_type=F32)
        acc = part if acc is None else acc + part
    o_ref[0] = x_ref[0] + g_ref[0] * acc


def _matmul_residual(pairs, x, mod, g_chunk, tm=1024, tn=512):
    b, s, d = x.shape
    tm, tn = min(tm, s), min(tn, d)
    in_specs, args = [], []
    for a, w in pairs:
        k = a.shape[2]
        in_specs += [pl.BlockSpec((1, tm, k), lambda bi, i, j: (bi, i, 0)),
                     pl.BlockSpec((k, tn), lambda bi, i, j: (0, j))]
        args += [a, w]
    nj = d // tn
    in_specs += [pl.BlockSpec((1, tm, tn), lambda bi, i, j: (bi, i, j)),
                 pl.BlockSpec((1, 1, tn), lambda bi, i, j: (bi, 0, g_chunk * nj + j))]
    return pl.pallas_call(
        functools.partial(_mm_res_kernel, n_pairs=len(pairs)),
        grid=(b, s // tm, nj),
        in_specs=in_specs,
        out_specs=pl.BlockSpec((1, tm, tn), lambda bi, i, j: (bi, i, j)),
        out_shape=jax.ShapeDtypeStruct((b, s, d), x.dtype),
        compiler_params=_params(("arbitrary", "arbitrary", "arbitrary")),
        name="matmul_residual",
    )(*args, x, mod)


def _lane_cumsum(x):
    n = x.shape[-1]
    lane = lax.broadcasted_iota(I32, x.shape, x.ndim - 1)
    d = 1
    while d < n:
        x = x + jnp.where(lane >= d, pltpu.roll(x, d, x.ndim - 1), 0.0)
        d *= 2
    return x


def _even_prep_kernel(z_ref, kvn_ref, fb_ref, ckv_ref, negcum_ref, ike_ref, iko_ref, iw_ref,
                      *, rank, fox_heads, idx_heads):
    z = z_ref[0]
    ckv_ref[0] = _rms(z[:, :rank], kvn_ref[...]).astype(BF16)
    f_t = z[:, rank:rank + LANES].T
    logf = jax.nn.log_sigmoid(f_t[:fox_heads] + fb_ref[...])
    negcum_ref[0] = -_lane_cumsum(logf)
    ik = z[:, rank + LANES:rank + LANES + IDX_DIM].astype(BF16)
    zero = jnp.zeros_like(ik)
    ike_ref[0] = jnp.concatenate([ik, zero], axis=1)
    iko_ref[0] = jnp.concatenate([zero, ik], axis=1)
    iw_ref[0] = z[:, rank + 2 * LANES:rank + 2 * LANES + idx_heads] * (idx_heads ** -0.5) * (IDX_DIM ** -0.5)


def _even_prep(z_s, kv_norm, fox_fb, idx_heads):
    b, s, w = z_s.shape
    rank = kv_norm.shape[0]
    fh = fox_fb.shape[0]
    outs = (
        jax.ShapeDtypeStruct((b, s, rank), BF16),
        jax.ShapeDtypeStruct((b, fh, s), F32),
        jax.ShapeDtypeStruct((b, s, LANES), BF16),
        jax.ShapeDtypeStruct((b, s, LANES), BF16),
        jax.ShapeDtypeStruct((b, s, idx_heads), F32),
    )
    return pl.pallas_call(
        functools.partial(_even_prep_kernel, rank=rank, fox_heads=fh, idx_heads=idx_heads),
        grid=(b,),
        in_specs=[
            pl.BlockSpec((1, s, w), lambda bi: (bi, 0, 0)),
            pl.BlockSpec((1, rank), lambda bi: (0, 0)),
            pl.BlockSpec((fh, 1), lambda bi: (0, 0)),
        ],
        out_specs=(
            pl.BlockSpec((1, s, rank), lambda bi: (bi, 0, 0)),
            pl.BlockSpec((1, fh, s), lambda bi: (bi, 0, 0)),
            pl.BlockSpec((1, s, LANES), lambda bi: (bi, 0, 0)),
            pl.BlockSpec((1, s, LANES), lambda bi: (bi, 0, 0)),
            pl.BlockSpec((1, s, idx_heads), lambda bi: (bi, 0, 0)),
        ),
        out_shape=outs,
        compiler_params=_params(("arbitrary",)),
        name="even_prep",
    )(z_s, kv_norm.reshape(1, rank), fox_fb.reshape(fh, 1))


def _fox_kernel(q_ref, k_ref, v_ref, nc_ref, o_ref, *, heads, tq):
    s_len = q_ref.shape[1]
    scale = HEAD_DIM ** -0.5
    lower = lax.broadcasted_iota(I32, (tq, tq), 0) >= lax.broadcasted_iota(I32, (tq, tq), 1)
    head_cols = [slice(hh * HEAD_DIM, (hh + 1) * HEAD_DIM) for hh in range(heads)]

    def q_body(qi, _):
        q0 = pl.multiple_of(qi * tq, tq)
        qs = [q_ref[0, pl.ds(q0, tq), hs] for hs in head_cols]

        def k_step(kj, carries, diagonal):
            k0 = pl.multiple_of(kj * tq, tq)
            out = []
            for hh, hs in enumerate(head_cols):
                m, l, acc = carries[hh]
                k = k_ref[0, pl.ds(k0, tq), hs]
                v = v_ref[0, pl.ds(k0, tq), hs]
                s = lax.dot_general(qs[hh], k, NT_DIMS, preferred_element_type=F32) * scale
                s = s + nc_ref[0, hh, pl.ds(kj, 1), :]
                if diagonal:
                    s = jnp.where(lower, s, -jnp.inf)
                m_new = jnp.maximum(m, jnp.max(s, axis=-1, keepdims=True))
                alpha = jnp.exp(m - m_new)
                p = jnp.exp(s - m_new)
                l = alpha * l + jnp.sum(p, axis=-1, keepdims=True)
                acc = alpha * acc + jnp.dot(p.astype(BF16), v, preferred_element_type=F32)
                out.append((m_new, l, acc))
            return tuple(out)

        init = tuple((jnp.full((tq, 1), -jnp.inf, F32), jnp.zeros((tq, 1), F32),
                      jnp.zeros((tq, HEAD_DIM), F32)) for _ in head_cols)
        carries = lax.fori_loop(0, qi, functools.partial(k_step, diagonal=False), init)
        carries = k_step(qi, carries, diagonal=True)
        for (m, l, acc), hs in zip(carries, head_cols):
            o_ref[0, pl.ds(q0, tq), hs] = (acc / l).astype(o_ref.dtype)
        return 0

    lax.fori_loop(0, s_len // tq, q_body, 0)


def _fox_attention(z_a, neg_cum, fox_heads, heads_per_step=2, tq=512):
    b, s, _ = z_a.shape
    tq = tk = min(tq, s)
    hp = heads_per_step
    wblk = hp * HEAD_DIM
    nblk = fox_heads // hp
    nc = neg_cum.reshape(b, fox_heads, s // tk, tk)
    return pl.pallas_call(
        functools.partial(_fox_kernel, heads=hp, tq=tq),
        grid=(b, nblk),
        in_specs=[
            pl.BlockSpec((1, s, wblk), lambda bi, h: (bi, 0, h)),
            pl.BlockSpec((1, s, wblk), lambda bi, h: (bi, 0, nblk + h)),
            pl.BlockSpec((1, s, wblk), lambda bi, h: (bi, 0, 2 * nblk + h)),
            pl.BlockSpec((1, hp, s // tk, tk), lambda bi, h: (bi, h, 0, 0)),
        ],
        out_specs=pl.BlockSpec((1, s, wblk), lambda bi, h: (bi, 0, h)),
        out_shape=jax.ShapeDtypeStruct((b, s, fox_heads * HEAD_DIM), BF16),
        compiler_params=_params(("arbitrary", "arbitrary")),
        name="fox_attention",
    )(z_a, z_a, z_a, nc)


DSA_TQ = 256
INT_MIN = int(np.iinfo(np.int32).min)


def _t5_thresholds(max_d):
    max_exact = N_BUCKETS // 2
    d = np.arange(max_d)
    df = np.maximum(d, 1).astype(np.float32)
    large = max_exact + (np.log(df / np.float32(max_exact)) / np.float32(math.log(MAX_DISTANCE / max_exact))
                         * np.float32(N_BUCKETS - max_exact)).astype(np.int32)
    table = np.where(d < max_exact, d, np.minimum(large, N_BUCKETS - 1))
    assert np.all(np.diff(table) >= 0) and table[-1] == N_BUCKETS - 1
    return [int(np.argmax(table >= b)) for b in range(N_BUCKETS)]


def _bias_tile_kernel(rb_ref, o_ref, *, tq, thresholds):
    h = pl.program_id(0)
    a = lax.broadcasted_iota(I32, (tq, 2 * tq), 0)
    c = lax.broadcasted_iota(I32, (tq, 2 * tq), 1)
    d = tq + a - c
    val = jnp.full((tq, 2 * tq), rb_ref[0, h], F32)
    for b in range(1, N_BUCKETS):
        val = jnp.where(d >= thresholds[b], rb_ref[b, h], val)
    o_ref[0] = val


def _bias_tiles(rel_bias, tq):
    heads = rel_bias.shape[1]
    thresholds = _t5_thresholds(2 * tq)
    assert thresholds[N_BUCKETS - 1] <= tq + 1
    return pl.pallas_call(
        functools.partial(_bias_tile_kernel, tq=tq, thresholds=thresholds),
        grid=(heads,),
        in_specs=[pl.BlockSpec(memory_space=pltpu.SMEM)],
        out_specs=pl.BlockSpec((1, tq, 2 * tq), lambda h: (h, 0, 0)),
        out_shape=jax.ShapeDtypeStruct((heads, tq, 2 * tq), F32),
        compiler_params=_params(("arbitrary",)),
        name="bias_tiles",
    )(rel_bias)


def _dsa_kernel(rb_ref, qb_ref, iq_ref, iw_ref, ckv_ref, ike_ref, iko_ref, wuk_ref, wuv_ref, tile_ref,
                o_ref, key_sc, madd_sc, p_sc, *, heads, idx_heads, topk, tq, group, i):
    s_len, rank = ckv_ref.shape[1], ckv_ref.shape[2]
    scale = HEAD_DIM ** -0.5

    iq = iq_ref[0]
    iw = iw_ref[0]
    ike = ike_ref[0]
    iko = iko_ref[0]
    nch = idx_heads * IDX_DIM // LANES
    score = jnp.zeros((tq, s_len), F32)
    for g0 in range(0, nch, 2):
        cs = list(range(g0, min(g0 + 2, nch)))
        a = jnp.concatenate([iq[:, c * LANES:(c + 1) * LANES] for c in cs], axis=0)
        r_e = lax.dot_general(a, ike, NT_DIMS, preferred_element_type=F32)
        r_o = lax.dot_general(a, iko, NT_DIMS, preferred_element_type=F32)
        for n, c in enumerate(cs):
            rows = slice(n * tq, (n + 1) * tq)
            score = score + jnp.maximum(r_e[rows], 0.0) * iw[:, 2 * c:2 * c + 1]
            score = score + jnp.maximum(r_o[rows], 0.0) * iw[:, 2 * c + 1:2 * c + 2]

    t_idx = i * tq + lax.broadcasted_iota(I32, (tq, s_len), 0)
    s_idx = lax.broadcasted_iota(I32, (tq, s_len), 1)
    bits = pltpu.bitcast(score, I32)
    key = bits ^ ((bits >> 31) & 0x7FFFFFFF)
    key_sc[...] = jnp.where(t_idx >= s_idx, key, INT_MIN)

    def bit_step(it, lo):
        cand = lo + lax.shift_left(jnp.int32(1), 31 - it)
        cnt = jnp.sum(jnp.where(key_sc[...] >= cand, 1.0, 0.0), axis=-1, keepdims=True)
        return jnp.where(cnt >= topk, cand, lo)
    thr = lax.fori_loop(0, 32, bit_step, jnp.full((tq, 1), INT_MIN, I32))

    n_gt = jnp.sum(jnp.where(key_sc[...] > thr, 1.0, 0.0), axis=-1, keepdims=True)
    n_eq = jnp.sum(jnp.where(key_sc[...] == thr, 1.0, 0.0), axis=-1, keepdims=True)
    need = topk - n_gt
    nbits = (s_len - 1).bit_length()

    def idx_step(it, p):
        cand = p + lax.shift_left(jnp.int32(1), nbits - 1 - it)
        tied_below = jnp.where(key_sc[...] == thr, jnp.where(s_idx < cand, 1.0, 0.0), 0.0)
        cnt = jnp.sum(tied_below, axis=-1, keepdims=True)
        return jnp.where(cnt < need, cand, p)

    excess = jnp.where(jnp.logical_and(n_eq > need, thr != INT_MIN), 1, 0)
    p_last = lax.cond(jnp.max(excess) > 0,
                      lambda: lax.fori_loop(0, nbits, idx_step, jnp.zeros((tq, 1), I32)),
                      lambda: jnp.full((tq, 1), s_len, I32))

    key = key_sc[...]
    keep = jnp.where(key > thr, 0.0, jnp.where(key == thr, jnp.where(s_idx <= p_last, 0.0, -jnp.inf), -jnp.inf))
    madd_sc[...] = jnp.where(t_idx >= s_idx, keep, -jnp.inf)

    qb = qb_ref[0]
    ckv = ckv_ref[0]
    madd = madd_sc[...]
    n_far = max(s_len - 2 * tq, 0)
    for g in range(heads // group):
        hs = range(g * group, (g + 1) * group)
        qg = jnp.concatenate(
            [lax.dot_general(qb[:, h * HEAD_DIM:(h + 1) * HEAD_DIM], wuk_ref[h], NT_DIMS,
                             preferred_element_type=F32).astype(BF16) for h in hs], axis=0)
        sg = lax.dot_general(qg, ckv, NT_DIMS, preferred_element_type=F32) * scale
        sums = []
        for n, h in enumerate(hs):
            near = tile_ref[h][:, 2 * tq - (s_len - n_far):]
            if n_far:
                far = jnp.full((tq, n_far), rb_ref[N_BUCKETS - 1, h], F32)
                near = jnp.concatenate([far, near], axis=1)
            s = sg[n * tq:(n + 1) * tq] + near + madd
            m = jnp.max(s, axis=-1, keepdims=True)
            p = jnp.exp(s - m)
            sums.append(jnp.sum(p, axis=-1, keepdims=True))
            p_sc[n] = p.astype(BF16)
        og = jnp.dot(p_sc[...].reshape(group * tq, s_len), ckv, preferred_element_type=F32)
        for n, h in enumerate(hs):
            ol = (og[n * tq:(n + 1) * tq] / sums[n]).astype(BF16)
            o_ref[0, :, h * HEAD_DIM:(h + 1) * HEAD_DIM] = jnp.dot(
                ol, wuv_ref[h], preferred_element_type=F32).astype(o_ref.dtype)


def _dsa_attention(z_a, iw_s, ckv_n, ik_e, ik_o, w_uk, w_uv, rel_bias, fox_w, idx_heads, group=2):
    b, s, _ = z_a.shape
    heads, rank, _ = w_uk.shape
    tq = min(DSA_TQ, s)
    group = min(group, heads)
    assert s & (s - 1) == 0 and heads % group == 0
    dsa_w = heads * HEAD_DIM
    iq_w = idx_heads * IDX_DIM
    assert (3 * fox_w) % dsa_w == 0 and (3 * fox_w + dsa_w) % iq_w == 0
    qb_blk = 3 * fox_w // dsa_w
    iq_blk = (3 * fox_w + dsa_w) // iq_w
    topk = min(TOPK_MAX, s // 4)
    tiles = _bias_tiles(rel_bias, tq)
    outs = []
    for i in range(s // tq):
        sk = (i + 1) * tq
        const = dict(pipeline_mode=pl.Buffered(1))
        outs.append(pl.pallas_call(
            functools.partial(_dsa_kernel, heads=heads, idx_heads=idx_heads, topk=topk, tq=tq, group=group, i=i),
            grid=(b,),
            in_specs=[
                pl.BlockSpec(memory_space=pltpu.SMEM),
                pl.BlockSpec((1, tq, dsa_w), lambda bi, i=i: (bi, i, qb_blk)),
                pl.BlockSpec((1, tq, iq_w), lambda bi, i=i: (bi, i, iq_blk)),
                pl.BlockSpec((1, tq, idx_heads), lambda bi, i=i: (bi, i, 0)),
                pl.BlockSpec((1, sk, rank), lambda bi: (bi, 0, 0)),
                pl.BlockSpec((1, sk, LANES), lambda bi: (bi, 0, 0)),
                pl.BlockSpec((1, sk, LANES), lambda bi: (bi, 0, 0)),
                pl.BlockSpec((heads, rank, HEAD_DIM), lambda bi: (0, 0, 0), **const),
                pl.BlockSpec((heads, rank, HEAD_DIM), lambda bi: (0, 0, 0), **const),
                pl.BlockSpec((heads, tq, 2 * tq), lambda bi: (0, 0, 0), **const),
            ],
            out_specs=pl.BlockSpec((1, tq, dsa_w), lambda bi: (bi, 0, 0)),
            out_shape=jax.ShapeDtypeStruct((b, tq, dsa_w), BF16),
            scratch_shapes=[
                pltpu.VMEM((tq, sk), I32),
                pltpu.VMEM((tq, sk), F32),
                pltpu.VMEM((group, tq, sk), BF16),
            ],
            compiler_params=_params(("arbitrary",)),
            name=f"dsa_attention_{i}",
        )(rel_bias, z_a, z_a, iw_s, ckv_n, ik_e, ik_o, w_uk, w_uv, tiles))
    return jnp.concatenate(outs, axis=1)


def _shortconv_kernel(a_ref, wb_ref, wc_ref, wu_ref, cw_ref, o_ref):
    a = a_ref[0]
    bg = jnp.dot(a, wb_ref[...], preferred_element_type=F32)
    cg = jnp.dot(a, wc_ref[...], preferred_element_type=F32)
    u = jnp.dot(a, wu_ref[...], preferred_element_type=F32)
    p = cg * u
    row = lax.broadcasted_iota(I32, p.shape, 0)
    p1 = jnp.where(row >= 1, pltpu.roll(p, 1, 0), 0.0)
    p2 = jnp.where(row >= 2, pltpu.roll(p, 2, 0), 0.0)
    cw = cw_ref[...]
    y = p2 * cw[0:1] + p1 * cw[1:2] + p * cw[2:3]
    o_ref[0] = (bg * y).astype(o_ref.dtype)


def _shortconv(h, w_in, conv_w, tn=256):
    b, s, d = h.shape
    assert conv_w.shape[0] == 3
    tn = min(tn, d)
    nj = d // tn
    return pl.pallas_call(
        _shortconv_kernel,
        grid=(b, nj),
        in_specs=[
            pl.BlockSpec((1, s, d), lambda bi, j: (bi, 0, 0), pipeline_mode=pl.Buffered(1)),
            pl.BlockSpec((d, tn), lambda bi, j: (0, j)),
            pl.BlockSpec((d, tn), lambda bi, j: (0, nj + j)),
            pl.BlockSpec((d, tn), lambda bi, j: (0, 2 * nj + j)),
            pl.BlockSpec((3, tn), lambda bi, j: (0, j)),
        ],
        out_specs=pl.BlockSpec((1, s, tn), lambda bi, j: (bi, 0, j)),
        out_shape=jax.ShapeDtypeStruct((b, s, d), BF16),
        compiler_params=_params(("arbitrary", "arbitrary")),
        name="shortconv",
    )(h, w_in, w_in, w_in, conv_w)


def _pack_bf16_pairs(xb):
    half = xb.shape[1] // 2
    bits = pltpu.bitcast(xb.astype(F32), U32)
    return (bits[:, half:] & jnp.uint32(0xFFFF0000)) | (bits[:, :half] >> 16)


def _unpack_bf16_pairs(words):
    lo = pltpu.bitcast(words << 16, F32).astype(BF16)
    hi = pltpu.bitcast(words & jnp.uint32(0xFFFF0000), F32).astype(BF16)
    return jnp.concatenate([lo, hi], axis=1)


def _router_kernel(x_ref, g_ref, sc_ref, sh_ref, rw_ref, rb_ref, h_ref, meta_ref, wcol_ref, cnt_ref,
                   base_sc, *, n_exp):
    epg = n_exp // N_GROUPS
    first = jnp.logical_and(pl.program_id(0) == 0, pl.program_id(1) == 0)

    @pl.when(first)
    def _():
        base_sc[...] = jnp.zeros_like(base_sc)

    h = _rms(x_ref[0], g_ref[...]) * (1.0 + sc_ref[0]) + sh_ref[0]
    t, d = h.shape
    hb = h.astype(BF16)
    h_ref[0] = _pack_bf16_pairs(hb)

    logits_t = lax.dot_general(rw_ref[...], hb, NT_DIMS, preferred_element_type=F32)
    aff = jax.nn.sigmoid(logits_t[:n_exp])
    sel = aff + rb_ref[...]
    srow = [sel[e:e + 1] for e in range(n_exp)]
    arow = [aff[e:e + 1] for e in range(n_exp)]

    gscore = []
    for g in range(N_GROUPS):
        a0, a1, a2, a3 = srow[g * epg:(g + 1) * epg]
        hi1, lo1 = jnp.maximum(a0, a1), jnp.minimum(a0, a1)
        hi2, lo2 = jnp.maximum(a2, a3), jnp.minimum(a2, a3)
        gscore.append(jnp.maximum(hi1, hi2) + jnp.maximum(jnp.minimum(hi1, hi2), jnp.maximum(lo1, lo2)))
    grp = jnp.zeros((1, t), I32)
    best = gscore[0]
    for g in range(1, N_GROUPS):
        take = gscore[g] > best
        grp = jnp.where(take, g, grp)
        best = jnp.where(take, gscore[g], best)

    def pick(rows_, idx, n):
        out = rows_[0]
        for k in range(1, n):
            out = jnp.where(idx == k, rows_[k], out)
        return out

    in_s = [pick([srow[g * epg + k] for g in range(N_GROUPS)], grp, N_GROUPS) for k in range(epg)]
    in_a = [pick([arow[g * epg + k] for g in range(N_GROUPS)], grp, N_GROUPS) for k in range(epg)]

    i1 = jnp.zeros((1, t), I32)
    v1 = in_s[0]
    for k in range(1, epg):
        take = in_s[k] > v1
        i1 = jnp.where(take, k, i1)
        v1 = jnp.where(take, in_s[k], v1)
    i2 = jnp.where(i1 == 0, 1, 0)
    v2 = jnp.where(i1 == 0, in_s[1], in_s[0])
    for k in range(1, epg):
        take = jnp.logical_and(i1 != k, jnp.logical_and(i2 != k, in_s[k] > v2))
        i2 = jnp.where(take, k, i2)
        v2 = jnp.where(take, in_s[k], v2)
    e1 = grp * epg + i1
    e2 = grp * epg + i2
    a1 = pick(in_a, i1, epg)
    a2 = pick(in_a, i2, epg)
    denom = a1 + a2
    w1 = a1 / denom
    w2 = a2 / denom

    eiota = lax.broadcasted_iota(I32, (n_exp, t), 0)
    oh1 = jnp.where(eiota == e1, 1.0, 0.0)
    oh2 = jnp.where(eiota == e2, 1.0, 0.0)
    oh = oh1 + oh2
    before = lax.broadcasted_iota(I32, (t, t), 0) < lax.broadcasted_iota(I32, (t, t), 1)
    excl = jnp.dot(oh.astype(BF16), jnp.where(before, 1.0, 0.0).astype(BF16), preferred_element_type=F32)
    rank = base_sc[:, 0:1] + excl
    r1 = jnp.sum(oh1 * rank, axis=0, keepdims=True).astype(I32)
    r2 = jnp.sum(oh2 * rank, axis=0, keepdims=True).astype(I32)
    base_sc[...] = base_sc[...] + jnp.sum(oh, axis=1, keepdims=True)

    meta_ref[...] = jnp.concatenate([e1, e2, r1, r2, jnp.zeros((4, t), I32)], axis=0)
    wrows = jnp.concatenate([w1, w2, jnp.zeros((LANES - 2, t), F32)], axis=0)
    wcol_ref[...] = wrows.T
    cnt_ref[...] = base_sc[...].astype(I32)


def _router(x, g, mod, sc_chunk, sh_chunk, rw_pad, router_b, tt=256):
    b, s, d = x.shape
    n_exp = router_b.shape[0]
    assert n_exp // N_GROUPS == 4 and TOP_K == 2
    tt = min(tt, s)
    nt = s // tt
    n = b * s
    outs = (
        jax.ShapeDtypeStruct((b, s, d // 2), U32),
        jax.ShapeDtypeStruct((8, n), I32),
        jax.ShapeDtypeStruct((n, LANES), F32),
        jax.ShapeDtypeStruct((n_exp, LANES), I32),
    )
    return pl.pallas_call(
        functools.partial(_router_kernel, n_exp=n_exp),
        grid=(b, nt),
        in_specs=[
            pl.BlockSpec((1, tt, d), lambda bi, i: (bi, i, 0)),
            pl.BlockSpec((1, d), lambda bi, i: (0, 0)),
            _mod_spec(d, sc_chunk),
            _mod_spec(d, sh_chunk),
            pl.BlockSpec((LANES, d), lambda bi, i: (0, 0)),
            pl.BlockSpec((n_exp, 1), lambda bi, i: (0, 0)),
        ],
        out_specs=(
            pl.BlockSpec((1, tt, d // 2), lambda bi, i: (bi, i, 0)),
            pl.BlockSpec((8, tt), lambda bi, i: (0, bi * nt + i)),
            pl.BlockSpec((tt, LANES), lambda bi, i: (bi * nt + i, 0)),
            pl.BlockSpec((n_exp, LANES), lambda bi, i: (0, 0)),
        ),
        out_shape=outs,
        scratch_shapes=[pltpu.VMEM((n_exp, LANES), F32)],
        compiler_params=_params(("arbitrary", "arbitrary")),
        name="router",
    )(x, g.reshape(1, d), mod, mod, rw_pad, router_b.reshape(n_exp, 1))


def _row(ref, r):
    return ref.at[pl.ds(r, 1), :]


def _dispatch_kernel(dest_ref, fill_lo_ref, fill_hi_ref, h_ref, buf_hbm, zero_sc, sem, *, n_tok, n_exp):
    chunk = h_ref.shape[0]
    ci = pl.program_id(0)

    @pl.when(ci == 0)
    def _():
        zero_sc[...] = jnp.zeros_like(zero_sc)
        for e in range(n_exp + 1):
            lo, hi = fill_lo_ref[e], fill_hi_ref[e]

            def zfill(r, _):
                pltpu.make_async_copy(_row(zero_sc, 0), _row(buf_hbm, r), sem).start()
                return 0
            lax.fori_loop(lo, hi, zfill, 0)

            def zdrain(r, _):
                pltpu.make_async_copy(_row(zero_sc, 0), _row(buf_hbm, 0), sem).wait()
                return 0
            lax.fori_loop(lo, hi, zdrain, 0)

    def issue(t, _):
        for k in range(TOP_K):
            a = k * n_tok + ci * chunk + t
            pltpu.make_async_copy(_row(h_ref, t), _row(buf_hbm, dest_ref[a]), sem).start()
        return 0
    lax.fori_loop(0, chunk, issue, 0, unroll=DMA_ISSUE_UNROLL)

    def drain(t, _):
        for k in range(TOP_K):
            pltpu.make_async_copy(_row(h_ref, 0), _row(buf_hbm, 0), sem).wait()
        return 0
    lax.fori_loop(0, chunk, drain, 0)


def _dispatch(h_rows, dest, fill_lo, fill_hi, n_rows, chunk=256):
    n, w = h_rows.shape
    n_exp = fill_lo.shape[0] - 1
    chunk = min(chunk, n)
    return pl.pallas_call(
        functools.partial(_dispatch_kernel, n_tok=n, n_exp=n_exp),
        grid_spec=pltpu.PrefetchScalarGridSpec(
            num_scalar_prefetch=3,
            grid=(n // chunk,),
            in_specs=[pl.BlockSpec((chunk, w), lambda i, *_: (i, 0))],
            out_specs=pl.BlockSpec(memory_space=pl.ANY),
            scratch_shapes=[pltpu.VMEM((8, w), h_rows.dtype), pltpu.SemaphoreType.DMA(())],
        ),
        out_shape=jax.ShapeDtypeStruct((n_rows, w), h_rows.dtype),
        compiler_params=_params(("arbitrary",)),
        name="dispatch",
    )(dest, fill_lo, fill_hi, h_rows)


def _expert_kernel(blk_e_ref, nused_ref, x_ref, wg_ref, wu_ref, wd_ref, o_ref):
    @pl.when(pl.program_id(0) < nused_ref[0])
    def _():
        xs = _unpack_bf16_pairs(x_ref[...])
        gate = jnp.dot(xs, wg_ref[0], preferred_element_type=F32)
        up = jnp.dot(xs, wu_ref[0], preferred_element_type=F32)
        hid = (jax.nn.silu(gate) * up).astype(BF16)
        o_ref[...] = jnp.dot(hid, wd_ref[0], preferred_element_type=F32)

    @pl.when(pl.program_id(0) >= nused_ref[0])
    def _():
        o_ref[...] = jnp.zeros_like(o_ref)


def _experts(buf, blk_e, n_used, layer, w_gate, w_up, w_down):
    n_rows, half = buf.shape
    _, n_exp, d, f = w_gate.shape
    nb = n_rows // MOE_BLOCK

    def row_map(bi, blk_e_ref, nused_ref):
        return (jnp.minimum(bi, nused_ref[0] - 1), 0)

    def w_map(bi, blk_e_ref, nused_ref):
        return (layer, blk_e_ref[bi], 0, 0)

    return pl.pallas_call(
        _expert_kernel,
        grid_spec=pltpu.PrefetchScalarGridSpec(
            num_scalar_prefetch=2,
            grid=(nb,),
            in_specs=[
                pl.BlockSpec((MOE_BLOCK, half), row_map),
                pl.BlockSpec((None, 1, d, f), w_map, pipeline_mode=pl.Buffered(1)),
                pl.BlockSpec((None, 1, d, f), w_map, pipeline_mode=pl.Buffered(1)),
                pl.BlockSpec((None, 1, f, d), w_map, pipeline_mode=pl.Buffered(1)),
            ],
            out_specs=pl.BlockSpec((MOE_BLOCK, d), lambda bi, *_: (bi, 0)),
        ),
        out_shape=jax.ShapeDtypeStruct((n_rows, d), F32),
        compiler_params=_params(("arbitrary",)),
        name="experts",
    )(blk_e, n_used, buf, w_gate, w_up, w_down)


def _combine_kernel(dest_ref, x_ref, g_ref, wcol_ref, ob_hbm, ng_ref, *rest, nt, n_tok, last):
    if last:
        o_ref, gbuf, sem = rest
    else:
        sc_ref, sh_ref, o_ref, h_ref, gbuf, sem = rest
    tt = x_ref.shape[1]
    n_steps = pl.num_programs(0) * nt
    step = pl.program_id(0) * nt + pl.program_id(1)
    slot = step % 2
    nxt = jnp.minimum(step + 1, n_steps - 1)

    def issue(step_idx, slot_idx, t):
        for k in range(TOP_K):
            a = k * n_tok + step_idx * tt + t
            pltpu.make_async_copy(_row(ob_hbm, dest_ref[a]), _row(gbuf.at[slot_idx, k], t),
                                  sem.at[slot_idx]).start()

    def drain(slot_idx):
        def body(t, _):
            for k in range(TOP_K):
                pltpu.make_async_copy(_row(ob_hbm, 0), _row(gbuf.at[0, 0], 0), sem.at[slot_idx]).wait()
            return 0
        lax.fori_loop(0, tt, body, 0)

    @pl.when(step == 0)
    def _():
        def first(t, _):
            issue(0, 0, t)
            return 0
        lax.fori_loop(0, tt, first, 0, unroll=DMA_ISSUE_UNROLL)

    drain(slot)

    def chunk(c, _):
        r0 = pl.multiple_of(c * COMBINE_ROWS, COMBINE_ROWS)
        for u in range(COMBINE_ROWS):
            issue(nxt, 1 - slot, r0 + u)
        rows = pl.ds(r0, COMBINE_ROWS)
        y = gbuf[slot, 0, rows, :] * wcol_ref[rows, 0:1] + gbuf[slot, 1, rows, :] * wcol_ref[rows, 1:2]
        x_new = x_ref[0, rows, :] + g_ref[0] * y
        if last:
            o_ref[0, rows, :] = _rms(x_new, ng_ref[...])
        else:
            o_ref[0, rows, :] = x_new
            h_ref[0, rows, :] = (_rms(x_new, ng_ref[...]) * (1.0 + sc_ref[0]) + sh_ref[0]).astype(h_ref.dtype)
        return 0
    lax.fori_loop(0, tt // COMBINE_ROWS, chunk, 0)

    @pl.when(step == n_steps - 1)
    def _():
        drain(1 - slot)


def _combine(x, mod, g_chunk, wcol, out_buf, dest, next_g, next_mod, tt=256):
    b, s, d = x.shape
    tt = min(tt, s)
    nt = s // tt
    last = next_mod is None
    row_spec = pl.BlockSpec((1, tt, d), lambda bi, i, *_: (bi, i, 0))
    in_specs = [
        row_spec,
        pl.BlockSpec((1, 1, d), lambda bi, i, *_: (bi, 0, g_chunk)),
        pl.BlockSpec((tt, LANES), lambda bi, i, *_: (bi * nt + i, 0)),
        pl.BlockSpec(memory_space=pl.ANY),
        pl.BlockSpec((1, d), lambda bi, i, *_: (0, 0)),
    ]
    args = [x, mod, wcol, out_buf, next_g.reshape(1, d)]
    if last:
        out_specs, out_shape = row_spec, jax.ShapeDtypeStruct((b, s, d), x.dtype)
    else:
        in_specs += [_mod_spec(d, 1), _mod_spec(d, 0)]
        args += [next_mod, next_mod]
        out_specs = (row_spec, row_spec)
        out_shape = (jax.ShapeDtypeStruct((b, s, d), x.dtype), jax.ShapeDtypeStruct((b, s, d), BF16))
    return pl.pallas_call(
        functools.partial(_combine_kernel, nt=nt, n_tok=b * s, last=last),
        grid_spec=pltpu.PrefetchScalarGridSpec(
            num_scalar_prefetch=1,
            grid=(b, nt),
            in_specs=in_specs,
            out_specs=out_specs,
            scratch_shapes=[pltpu.VMEM((2, TOP_K, tt, d), F32), pltpu.SemaphoreType.DMA((2,))],
        ),
        out_shape=out_shape,
        compiler_params=_params(("arbitrary", "arbitrary")),
        name="combine",
    )(dest, *args)


def _moe(x, g, mod, rw_pad, router_b, layer, w_gate, w_up, w_down, next_g, next_mod):
    b, s, d = x.shape
    n = b * s
    n_exp = router_b.shape[0]
    h_rows, meta, wcol, cnt = _router(x, g, mod, 4, 3, rw_pad, router_b)
    counts = cnt[:, 0]
    padded = (counts + MOE_BLOCK - 1) // MOE_BLOCK * MOE_BLOCK
    pad_end = jnp.cumsum(padded)
    start = pad_end - padded
    nb = (n * TOP_K + n_exp * (MOE_BLOCK - 1) + MOE_BLOCK - 1) // MOE_BLOCK
    blk_first = jnp.arange(nb, dtype=I32) * MOE_BLOCK
    blk_e = jnp.minimum(jnp.sum(blk_first[:, None] >= pad_end[None, :], axis=1), n_exp - 1).astype(I32)
    n_used = (pad_end[-1:] // MOE_BLOCK).astype(I32)
    is_e = meta[0:2, :, None] == jnp.arange(n_exp, dtype=I32)
    dest = (meta[2:4] + jnp.sum(jnp.where(is_e, start.astype(I32), 0), axis=-1)).reshape(-1)
    n_rows = nb * MOE_BLOCK
    fill_lo = jnp.concatenate([start + counts, pad_end[-1:]]).astype(I32)
    fill_hi = jnp.concatenate([pad_end, jnp.full((1,), n_rows, pad_end.dtype)]).astype(I32)
    buf = _dispatch(h_rows.reshape(n, d // 2), dest, fill_lo, fill_hi, n_rows)
    out_buf = _experts(buf, blk_e, n_used, layer, w_gate, w_up, w_down)
    return _combine(x, mod, 5, wcol, out_buf, dest, next_g, next_mod)


def _pad_cols(w, width):
    return jnp.pad(w, ((0, 0), (0, width - w.shape[1])))


def kernel(x, c, ada_w, ada_b, norm1, norm2, w_in_ab, fox_fb, kv_norm, w_uk, w_uv, w_out_ab, w_in_c, conv_w, w_out_c, rel_bias, router_w, router_b, exp_gate, exp_up, exp_down, final_norm):
    depth, d = norm1.shape
    b, s, _ = x.shape
    fox_heads = fox_fb.shape[1]
    dsa_heads, rank = w_uk.shape[1], w_uk.shape[2]
    fox_w, dsa_w = fox_heads * HEAD_DIM, dsa_heads * HEAD_DIM
    even_cols = w_in_ab.shape[2]
    idx_heads = (even_cols - 3 * fox_w - fox_heads - dsa_w - rank - IDX_DIM) // (IDX_DIM + 1)
    iq_w = idx_heads * IDX_DIM
    o_fa = 3 * fox_w
    o_qb = o_fa + fox_heads
    o_ckv = o_qb + dsa_w
    o_iq = o_ckv + rank
    o_ik = o_iq + iq_w
    o_iw = o_ik + IDX_DIM
    assert o_iw + idx_heads == even_cols and fox_heads <= LANES and idx_heads <= LANES

    mods = _ada_mod(c, ada_w, ada_b)
    rw_pad = _pad_cols(router_w, LANES).T.astype(BF16)
    experts = (exp_gate.astype(BF16), exp_up.astype(BF16), exp_down.astype(BF16))
    mod_of = [mods[l].reshape(b, 1, 6 * d) for l in range(depth)]
    h = _norm_mod(x, norm1[0], mod_of[0], 1, 0)
    for l in range(depth):
        mod = mod_of[l]
        i = l // 2
        if l % 2 == 0:
            w = w_in_ab[i]
            w_a = jnp.concatenate([w[:, :o_fa], w[:, o_qb:o_ckv], w[:, o_iq:o_ik]], axis=1).astype(BF16)
            w_s = jnp.concatenate([w[:, o_ckv:o_iq], _pad_cols(w[:, o_fa:o_qb], LANES),
                                   _pad_cols(w[:, o_ik:o_iw], LANES), _pad_cols(w[:, o_iw:], LANES)],
                                  axis=1).astype(BF16)
            z_a = _matmul(h, w_a, BF16)
            z_s = _matmul(h, w_s, F32)
            ckv_n, neg_cum, ik_e, ik_o, iw_s = _even_prep(z_s, kv_norm[i], fox_fb[i], idx_heads)
            o_a = _fox_attention(z_a, neg_cum, fox_heads)
            o_b = _dsa_attention(z_a, iw_s, ckv_n, ik_e, ik_o, w_uk[i].astype(BF16), w_uv[i].astype(BF16),
                                 rel_bias, fox_w, idx_heads)
            w_o = w_out_ab[i].astype(BF16)
            x = _matmul_residual([(o_a, w_o[:fox_w]), (o_b, w_o[fox_w:])], x, mod, 2)
        else:
            y = _shortconv(h, w_in_c[i].astype(BF16), conv_w[i])
            x = _matmul_residual([(y, w_out_c[i].astype(BF16))], x, mod, 2)
        if l + 1 < depth:
            x, h = _moe(x, norm2[l], mod, rw_pad, router_b, l, *experts, norm1[l + 1], mod_of[l + 1])
        else:
            x = _moe(x, norm2[l], mod, rw_pad, router_b, l, *experts, final_norm, None)
    return x
```

```python
import functools
import math

import numpy as np
import jax
import jax.numpy as jnp
from jax import lax
from jax.experimental import pallas as pl
from jax.experimental.pallas import tpu as pltpu

F32 = jnp.float32
BF16 = jnp.bfloat16
I32 = jnp.int32
U32 = jnp.uint32

HEAD_DIM = 128
IDX_DIM = 64
TOPK_MAX = 256
N_BUCKETS = 32
MAX_DISTANCE = 128
N_GROUPS = 4
TOP_K = 2
EPS = 1e-6
LANES = 128
VMEM_LIMIT = 56 * 1024 * 1024
MOE_BLOCK = 256
COMBINE_ROWS = 64
DMA_ISSUE_UNROLL = 8
NT_DIMS = (((1,), (1,)), ((), ()))


def _params(sem, vmem=VMEM_LIMIT):
    return pltpu.CompilerParams(dimension_semantics=sem, vmem_limit_bytes=vmem)


def _ada_kernel(c_ref, w_ref, b_ref, o_ref):
    c = c_ref[...]
    ca = (c * jax.nn.sigmoid(c)).astype(BF16)
    o_ref[0] = jnp.dot(ca, w_ref[0].astype(BF16), preferred_element_type=F32) + b_ref[0]


def _ada_mod(c, ada_w, ada_b):
    depth, d, n6 = ada_w.shape
    b = c.shape[0]
    tn = min(512, n6)
    return pl.pallas_call(
        _ada_kernel,
        grid=(depth, n6 // tn),
        in_specs=[
            pl.BlockSpec((b, d), lambda l, j: (0, 0)),
            pl.BlockSpec((1, d, tn), lambda l, j: (l, 0, j)),
            pl.BlockSpec((1, 1, tn), lambda l, j: (l, 0, j)),
        ],
        out_specs=pl.BlockSpec((1, b, tn), lambda l, j: (l, 0, j)),
        out_shape=jax.ShapeDtypeStruct((depth, b, n6), F32),
        compiler_params=_params(("arbitrary", "arbitrary")),
        name="ada_mod",
    )(c, ada_w, ada_b.reshape(depth, 1, n6))


def _rms(x, g):
    return x * lax.rsqrt(jnp.mean(x * x, axis=-1, keepdims=True) + EPS) * g


def _norm_mod_kernel(x_ref, g_ref, sc_ref, sh_ref, o_ref):
    y = _rms(x_ref[0], g_ref[...])
    o_ref[0] = (y * (1.0 + sc_ref[0]) + sh_ref[0]).astype(o_ref.dtype)


def _mod_spec(d, chunk):
    return pl.BlockSpec((1, 1, d), lambda b, *_: (b, 0, chunk))


def _norm_mod(x, g, mod, sc_chunk, sh_chunk, ts=256):
    b, s, d = x.shape
    ts = min(ts, s)
    return pl.pallas_call(
        _norm_mod_kernel,
        grid=(b, s // ts),
        in_specs=[
            pl.BlockSpec((1, ts, d), lambda bi, i: (bi, i, 0)),
            pl.BlockSpec((1, d), lambda bi, i: (0, 0)),
            _mod_spec(d, sc_chunk),
            _mod_spec(d, sh_chunk),
        ],
        out_specs=pl.BlockSpec((1, ts, d), lambda bi, i: (bi, i, 0)),
        out_shape=jax.ShapeDtypeStruct((b, s, d), BF16),
        compiler_params=_params(("arbitrary", "arbitrary")),
        name="norm_mod",
    )(x, g.reshape(1, d), mod, mod)


def _mm_kernel(a_ref, w_ref, o_ref):
    o_ref[0] = jnp.dot(a_ref[0], w_ref[...], preferred_element_type=F32).astype(o_ref.dtype)


def _pick_tile(n, target):
    if n <= target:
        return n
    return max(t for t in range(LANES, target + 1, LANES) if n % t == 0)


def _matmul(a, w, out_dtype, tm=1024, tn=1024):
    b, s, k = a.shape
    n = w.shape[1]
    tm, tn = min(tm, s), _pick_tile(n, tn)
    return pl.pallas_call(
        _mm_kernel,
        grid=(b, s // tm, n // tn),
        in_specs=[
            pl.BlockSpec((1, tm, k), lambda bi, i, j: (bi, i, 0)),
            pl.BlockSpec((k, tn), lambda bi, i, j: (0, j)),
        ],
        out_specs=pl.BlockSpec((1, tm, tn), lambda bi, i, j: (bi, i, j)),
        out_shape=jax.ShapeDtypeStruct((b, s, n), out_dtype),
        compiler_params=_params(("arbitrary", "arbitrary", "arbitrary")),
        name="matmul",
    )(a, w)


def _mm_res_kernel(*refs, n_pairs):
    x_ref, g_ref, o_ref = refs[2 * n_pairs:]
    acc = None
    for p in range(n_pairs):
        part = jnp.dot(refs[2 * p][0], refs[2 * p + 1][...], preferred_element_type=F32)
        acc = part if acc is None else acc + part
    o_ref[0] = x_ref[0] + g_ref[0] * acc


def _matmul_residual(pairs, x, mod, g_chunk, tm=1024, tn=512):
    b, s, d = x.shape
    tm, tn = min(tm, s), min(tn, d)
    in_specs, args = [], []
    for a, w in pairs:
        k = a.shape[2]
        in_specs += [pl.BlockSpec((1, tm, k), lambda bi, i, j: (bi, i, 0)),
                     pl.BlockSpec((k, tn), lambda bi, i, j: (0, j))]
        args += [a, w]
    nj = d // tn
    in_specs += [pl.BlockSpec((1, tm, tn), lambda bi, i, j: (bi, i, j)),
                 pl.BlockSpec((1, 1, tn), lambda bi, i, j: (bi, 0, g_chunk * nj + j))]
    return pl.pallas_call(
        functools.partial(_mm_res_kernel, n_pairs=len(pairs)),
        grid=(b, s // tm, nj),
        in_specs=in_specs,
        out_specs=pl.BlockSpec((1, tm, tn), lambda bi, i, j: (bi, i, j)),
        out_shape=jax.ShapeDtypeStruct((b, s, d), x.dtype),
        compiler_params=_params(("arbitrary", "arbitrary", "arbitrary")),
        name="matmul_residual",
    )(*args, x, mod)


def _lane_cumsum(x):
    n = x.shape[-1]
    lane = lax.broadcasted_iota(I32, x.shape, x.ndim - 1)
    d = 1
    while d < n:
        x = x + jnp.where(lane >= d, pltpu.roll(x, d, x.ndim - 1), 0.0)
        d *= 2
    return x


def _even_prep_kernel(z_ref, kvn_ref, fb_ref, ckv_ref, negcum_ref, ike_ref, iko_ref, iw_ref,
                      *, rank, fox_heads, idx_heads):
    z = z_ref[0]
    ckv_ref[0] = _rms(z[:, :rank], kvn_ref[...]).astype(BF16)
    f_t = z[:, rank:rank + LANES].T
    logf = jax.nn.log_sigmoid(f_t[:fox_heads] + fb_ref[...])
    negcum_ref[0] = -_lane_cumsum(logf)
    ik = z[:, rank + LANES:rank + LANES + IDX_DIM].astype(BF16)
    zero = jnp.zeros_like(ik)
    ike_ref[0] = jnp.concatenate([ik, zero], axis=1)
    iko_ref[0] = jnp.concatenate([zero, ik], axis=1)
    iw_ref[0] = z[:, rank + 2 * LANES:rank + 2 * LANES + idx_heads] * (idx_heads ** -0.5) * (IDX_DIM ** -0.5)


def _even_prep(z_s, kv_norm, fox_fb, idx_heads):
    b, s, w = z_s.shape
    rank = kv_norm.shape[0]
    fh = fox_fb.shape[0]
    outs = (
        jax.ShapeDtypeStruct((b, s, rank), BF16),
        jax.ShapeDtypeStruct((b, fh, s), F32),
        jax.ShapeDtypeStruct((b, s, LANES), BF16),
        jax.ShapeDtypeStruct((b, s, LANES), BF16),
        jax.ShapeDtypeStruct((b, s, idx_heads), F32),
    )
    return pl.pallas_call(
        functools.partial(_even_prep_kernel, rank=rank, fox_heads=fh, idx_heads=idx_heads),
        grid=(b,),
        in_specs=[
            pl.BlockSpec((1, s, w), lambda bi: (bi, 0, 0)),
            pl.BlockSpec((1, rank), lambda bi: (0, 0)),
            pl.BlockSpec((fh, 1), lambda bi: (0, 0)),
        ],
        out_specs=(
            pl.BlockSpec((1, s, rank), lambda bi: (bi, 0, 0)),
            pl.BlockSpec((1, fh, s), lambda bi: (bi, 0, 0)),
            pl.BlockSpec((1, s, LANES), lambda bi: (bi, 0, 0)),
            pl.BlockSpec((1, s, LANES), lambda bi: (bi, 0, 0)),
            pl.BlockSpec((1, s, idx_heads), lambda bi: (bi, 0, 0)),
        ),
        out_shape=outs,
        compiler_params=_params(("arbitrary",)),
        name="even_prep",
    )(z_s, kv_norm.reshape(1, rank), fox_fb.reshape(fh, 1))


def _fox_kernel(q_ref, k_ref, v_ref, nc_ref, o_ref, *, heads, tq):
    s_len = q_ref.shape[1]
    scale = HEAD_DIM ** -0.5
    lower = lax.broadcasted_iota(I32, (tq, tq), 0) >= lax.broadcasted_iota(I32, (tq, tq), 1)
    head_cols = [slice(hh * HEAD_DIM, (hh + 1) * HEAD_DIM) for hh in range(heads)]

    def q_body(qi, _):
        q0 = pl.multiple_of(qi * tq, tq)
        qs = [q_ref[0, pl.ds(q0, tq), hs] for hs in head_cols]

        def k_step(kj, carries, diagonal):
            k0 = pl.multiple_of(kj * tq, tq)
            out = []
            for hh, hs in enumerate(head_cols):
                m, l, acc = carries[hh]
                k = k_ref[0, pl.ds(k0, tq), hs]
                v = v_ref[0, pl.ds(k0, tq), hs]
                s = lax.dot_general(qs[hh], k, NT_DIMS, preferred_element_type=F32) * scale
                s = s + nc_ref[0, hh, pl.ds(kj, 1), :]
                if diagonal:
                    s = jnp.where(lower, s, -jnp.inf)
                m_new = jnp.maximum(m, jnp.max(s, axis=-1, keepdims=True))
                alpha = jnp.exp(m - m_new)
                p = jnp.exp(s - m_new)
                l = alpha * l + jnp.sum(p, axis=-1, keepdims=True)
                acc = alpha * acc + jnp.dot(p.astype(BF16), v, preferred_element_type=F32)
                out.append((m_new, l, acc))
            return tuple(out)

        init = tuple((jnp.full((tq, 1), -jnp.inf, F32), jnp.zeros((tq, 1), F32),
                      jnp.zeros((tq, HEAD_DIM), F32)) for _ in head_cols)
        carries = lax.fori_loop(0, qi, functools.partial(k_step, diagonal=False), init)
        carries = k_step(qi, carries, diagonal=True)
        for (m, l, acc), hs in zip(carries, head_cols):
            o_ref[0, pl.ds(q0, tq), hs] = (acc / l).astype(o_ref.dtype)
        return 0

    lax.fori_loop(0, s_len // tq, q_body, 0)


def _fox_attention(z_a, neg_cum, fox_heads, heads_per_step=2, tq=512):
    b, s, _ = z_a.shape
    tq = tk = min(tq, s)
    hp = heads_per_step
    wblk = hp * HEAD_DIM
    nblk = fox_heads // hp
    nc = neg_cum.reshape(b, fox_heads, s // tk, tk)
    return pl.pallas_call(
        functools.partial(_fox_kernel, heads=hp, tq=tq),
        grid=(b, nblk),
        in_specs=[
            pl.BlockSpec((1, s, wblk), lambda bi, h: (bi, 0, h)),
            pl.BlockSpec((1, s, wblk), lambda bi, h: (bi, 0, nblk + h)),
            pl.BlockSpec((1, s, wblk), lambda bi, h: (bi, 0, 2 * nblk + h)),
            pl.BlockSpec((1, hp, s // tk, tk), lambda bi, h: (bi, h, 0, 0)),
        ],
        out_specs=pl.BlockSpec((1, s, wblk), lambda bi, h: (bi, 0, h)),
        out_shape=jax.ShapeDtypeStruct((b, s, fox_heads * HEAD_DIM), BF16),
        compiler_params=_params(("arbitrary", "arbitrary")),
        name="fox_attention",
    )(z_a, z_a, z_a, nc)


DSA_TQ = 256
INT_MIN = int(np.iinfo(np.int32).min)


def _t5_thresholds(max_d):
    max_exact = N_BUCKETS // 2
    d = np.arange(max_d)
    df = np.maximum(d, 1).astype(np.float32)
    large = max_exact + (np.log(df / np.float32(max_exact)) / np.float32(math.log(MAX_DISTANCE / max_exact))
                         * np.float32(N_BUCKETS - max_exact)).astype(np.int32)
    table = np.where(d < max_exact, d, np.minimum(large, N_BUCKETS - 1))
    assert np.all(np.diff(table) >= 0) and table[-1] == N_BUCKETS - 1
    return [int(np.argmax(table >= b)) for b in range(N_BUCKETS)]


def _bias_tile_kernel(rb_ref, o_ref, *, tq, thresholds):
    h = pl.program_id(0)
    a = lax.broadcasted_iota(I32, (tq, 2 * tq), 0)
    c = lax.broadcasted_iota(I32, (tq, 2 * tq), 1)
    d = tq + a - c
    val = jnp.full((tq, 2 * tq), rb_ref[0, h], F32)
    for b in range(1, N_BUCKETS):
        val = jnp.where(d >= thresholds[b], rb_ref[b, h], val)
    o_ref[0] = val


def _bias_tiles(rel_bias, tq):
    heads = rel_bias.shape[1]
    thresholds = _t5_thresholds(2 * tq)
    assert thresholds[N_BUCKETS - 1] <= tq + 1
    return pl.pallas_call(
        functools.partial(_bias_tile_kernel, tq=tq, thresholds=thresholds),
        grid=(heads,),
        in_specs=[pl.BlockSpec(memory_space=pltpu.SMEM)],
        out_specs=pl.BlockSpec((1, tq, 2 * tq), lambda h: (h, 0, 0)),
        out_shape=jax.ShapeDtypeStruct((heads, tq, 2 * tq), F32),
        compiler_params=_params(("arbitrary",)),
        name="bias_tiles",
    )(rel_bias)


def _dsa_kernel(rb_ref, qb_ref, iq_ref, iw_ref, ckv_ref, ike_ref, iko_ref, wuk_ref, wuv_ref, tile_ref,
                o_ref, key_sc, madd_sc, p_sc, *, heads, idx_heads, topk, tq, group, i):
    s_len, rank = ckv_ref.shape[1], ckv_ref.shape[2]
    scale = HEAD_DIM ** -0.5

    iq = iq_ref[0]
    iw = iw_ref[0]
    ike = ike_ref[0]
    iko = iko_ref[0]
    nch = idx_heads * IDX_DIM // LANES
    score = jnp.zeros((tq, s_len), F32)
    for g0 in range(0, nch, 2):
        cs = list(range(g0, min(g0 + 2, nch)))
        a = jnp.concatenate([iq[:, c * LANES:(c + 1) * LANES] for c in cs], axis=0)
        r_e = lax.dot_general(a, ike, NT_DIMS, preferred_element_type=F32)
        r_o = lax.dot_general(a, iko, NT_DIMS, preferred_element_type=F32)
        for n, c in enumerate(cs):
            rows = slice(n * tq, (n + 1) * tq)
            score = score + jnp.maximum(r_e[rows], 0.0) * iw[:, 2 * c:2 * c + 1]
            score = score + jnp.maximum(r_o[rows], 0.0) * iw[:, 2 * c + 1:2 * c + 2]

    t_idx = i * tq + lax.broadcasted_iota(I32, (tq, s_len), 0)
    s_idx = lax.broadcasted_iota(I32, (tq, s_len), 1)
    bits = pltpu.bitcast(score, I32)
    key = bits ^ ((bits >> 31) & 0x7FFFFFFF)
    key_sc[...] = jnp.where(t_idx >= s_idx, key, INT_MIN)

    def bit_step(it, lo):
        cand = lo + lax.shift_left(jnp.int32(1), 31 - it)
        cnt = jnp.sum(jnp.where(key_sc[...] >= cand, 1.0, 0.0), axis=-1, keepdims=True)
        return jnp.where(cnt >= topk, cand, lo)
    thr = lax.fori_loop(0, 32, bit_step, jnp.full((tq, 1), INT_MIN, I32))

    n_gt = jnp.sum(jnp.where(key_sc[...] > thr, 1.0, 0.0), axis=-1, keepdims=True)
    n_eq = jnp.sum(jnp.where(key_sc[...] == thr, 1.0, 0.0), axis=-1, keepdims=True)
    need = topk - n_gt
    nbits = (s_len - 1).bit_length()

    def idx_step(it, p):
        cand = p + lax.shift_left(jnp.int32(1), nbits - 1 - it)
        tied_below = jnp.where(key_sc[...] == thr, jnp.where(s_idx < cand, 1.0, 0.0), 0.0)
        cnt = jnp.sum(tied_below, axis=-1, keepdims=True)
        return jnp.where(cnt < need, cand, p)

    excess = jnp.where(jnp.logical_and(n_eq > need, thr != INT_MIN), 1, 0)
    p_last = lax.cond(jnp.max(excess) > 0,
                      lambda: lax.fori_loop(0, nbits, idx_step, jnp.zeros((tq, 1), I32)),
                      lambda: jnp.full((tq, 1), s_len, I32))

    key = key_sc[...]
    keep = jnp.where(key > thr, 0.0, jnp.where(key == thr, jnp.where(s_idx <= p_last, 0.0, -jnp.inf), -jnp.inf))
    madd_sc[...] = jnp.where(t_idx >= s_idx, keep, -jnp.inf)

    qb = qb_ref[0]
    ckv = ckv_ref[0]
    madd = madd_sc[...]
    n_far = max(s_len - 2 * tq, 0)
    for g in range(heads // group):
        hs = range(g * group, (g + 1) * group)
        qg = jnp.concatenate(
            [lax.dot_general(qb[:, h * HEAD_DIM:(h + 1) * HEAD_DIM], wuk_ref[h], NT_DIMS,
                             preferred_element_type=F32).astype(BF16) for h in hs], axis=0)
        sg = lax.dot_general(qg, ckv, NT_DIMS, preferred_element_type=F32) * scale
        sums = []
        for n, h in enumerate(hs):
            near = tile_ref[h][:, 2 * tq - (s_len - n_far):]
            if n_far:
                far = jnp.full((tq, n_far), rb_ref[N_BUCKETS - 1, h], F32)
                near = jnp.concatenate([far, near], axis=1)
            s = sg[n * tq:(n + 1) * tq] + near + madd
            m = jnp.max(s, axis=-1, keepdims=True)
            p = jnp.exp(s - m)
            sums.append(jnp.sum(p, axis=-1, keepdims=True))
            p_sc[n] = p.astype(BF16)
        og = jnp.dot(p_sc[...].reshape(group * tq, s_len), ckv, preferred_element_type=F32)
        for n, h in enumerate(hs):
            ol = (og[n * tq:(n + 1) * tq] / sums[n]).astype(BF16)
            o_ref[0, :, h * HEAD_DIM:(h + 1) * HEAD_DIM] = jnp.dot(
                ol, wuv_ref[h], preferred_element_type=F32).astype(o_ref.dtype)


def _dsa_attention(z_a, iw_s, ckv_n, ik_e, ik_o, w_uk, w_uv, rel_bias, fox_w, idx_heads, group=2):
    b, s, _ = z_a.shape
    heads, rank, _ = w_uk.shape
    tq = min(DSA_TQ, s)
    group = min(group, heads)
    assert s & (s - 1) == 0 and heads % group == 0
    dsa_w = heads * HEAD_DIM
    iq_w = idx_heads * IDX_DIM
    assert (3 * fox_w) % dsa_w == 0 and (3 * fox_w + dsa_w) % iq_w == 0
    qb_blk = 3 * fox_w // dsa_w
    iq_blk = (3 * fox_w + dsa_w) // iq_w
    topk = min(TOPK_MAX, s // 4)
    tiles = _bias_tiles(rel_bias, tq)
    outs = []
    for i in range(s // tq):
        sk = (i + 1) * tq
        const = dict(pipeline_mode=pl.Buffered(1))
        outs.append(pl.pallas_call(
            functools.partial(_dsa_kernel, heads=heads, idx_heads=idx_heads, topk=topk, tq=tq, group=group, i=i),
            grid=(b,),
            in_specs=[
                pl.BlockSpec(memory_space=pltpu.SMEM),
                pl.BlockSpec((1, tq, dsa_w), lambda bi, i=i: (bi, i, qb_blk)),
                pl.BlockSpec((1, tq, iq_w), lambda bi, i=i: (bi, i, iq_blk)),
                pl.BlockSpec((1, tq, idx_heads), lambda bi, i=i: (bi, i, 0)),
                pl.BlockSpec((1, sk, rank), lambda bi: (bi, 0, 0)),
                pl.BlockSpec((1, sk, LANES), lambda bi: (bi, 0, 0)),
                pl.BlockSpec((1, sk, LANES), lambda bi: (bi, 0, 0)),
                pl.BlockSpec((heads, rank, HEAD_DIM), lambda bi: (0, 0, 0), **const),
                pl.BlockSpec((heads, rank, HEAD_DIM), lambda bi: (0, 0, 0), **const),
                pl.BlockSpec((heads, tq, 2 * tq), lambda bi: (0, 0, 0), **const),
            ],
            out_specs=pl.BlockSpec((1, tq, dsa_w), lambda bi: (bi, 0, 0)),
            out_shape=jax.ShapeDtypeStruct((b, tq, dsa_w), BF16),
            scratch_shapes=[
                pltpu.VMEM((tq, sk), I32),
                pltpu.VMEM((tq, sk), F32),
                pltpu.VMEM((group, tq, sk), BF16),
            ],
            compiler_params=_params(("arbitrary",)),
            name=f"dsa_attention_{i}",
        )(rel_bias, z_a, z_a, iw_s, ckv_n, ik_e, ik_o, w_uk, w_uv, tiles))
    return jnp.concatenate(outs, axis=1)


def _shortconv_kernel(a_ref, wb_ref, wc_ref, wu_ref, cw_ref, o_ref):
    a = a_ref[0]
    bg = jnp.dot(a, wb_ref[...], preferred_element_type=F32)
    cg = jnp.dot(a, wc_ref[...], preferred_element_type=F32)
    u = jnp.dot(a, wu_ref[...], preferred_element_type=F32)
    p = cg * u
    row = lax.broadcasted_iota(I32, p.shape, 0)
    p1 = jnp.where(row >= 1, pltpu.roll(p, 1, 0), 0.0)
    p2 = jnp.where(row >= 2, pltpu.roll(p, 2, 0), 0.0)
    cw = cw_ref[...]
    y = p2 * cw[0:1] + p1 * cw[1:2] + p * cw[2:3]
    o_ref[0] = (bg * y).astype(o_ref.dtype)


def _shortconv(h, w_in, conv_w, tn=256):
    b, s, d = h.shape
    assert conv_w.shape[0] == 3
    tn = min(tn, d)
    nj = d // tn
    return pl.pallas_call(
        _shortconv_kernel,
        grid=(b, nj),
        in_specs=[
            pl.BlockSpec((1, s, d), lambda bi, j: (bi, 0, 0), pipeline_mode=pl.Buffered(1)),
            pl.BlockSpec((d, tn), lambda bi, j: (0, j)),
            pl.BlockSpec((d, tn), lambda bi, j: (0, nj + j)),
            pl.BlockSpec((d, tn), lambda bi, j: (0, 2 * nj + j)),
            pl.BlockSpec((3, tn), lambda bi, j: (0, j)),
        ],
        out_specs=pl.BlockSpec((1, s, tn), lambda bi, j: (bi, 0, j)),
        out_shape=jax.ShapeDtypeStruct((b, s, d), BF16),
        compiler_params=_params(("arbitrary", "arbitrary")),
        name="shortconv",
    )(h, w_in, w_in, w_in, conv_w)


def _pack_bf16_pairs(xb):
    half = xb.shape[1] // 2
    bits = pltpu.bitcast(xb.astype(F32), U32)
    return (bits[:, half:] & jnp.uint32(0xFFFF0000)) | (bits[:, :half] >> 16)


def _unpack_bf16_pairs(words):
    lo = pltpu.bitcast(words << 16, F32).astype(BF16)
    hi = pltpu.bitcast(words & jnp.uint32(0xFFFF0000), F32).astype(BF16)
    return jnp.concatenate([lo, hi], axis=1)


def _router_kernel(x_ref, g_ref, sc_ref, sh_ref, rw_ref, rb_ref, h_ref, meta_ref, wcol_ref, cnt_ref,
                   base_sc, *, n_exp):
    epg = n_exp // N_GROUPS
    first = jnp.logical_and(pl.program_id(0) == 0, pl.program_id(1) == 0)

    @pl.when(first)
    def _():
        base_sc[...] = jnp.zeros_like(base_sc)

    h = _rms(x_ref[0], g_ref[...]) * (1.0 + sc_ref[0]) + sh_ref[0]
    t, d = h.shape
    hb = h.astype(BF16)
    h_ref[0] = _pack_bf16_pairs(hb)

    logits_t = lax.dot_general(rw_ref[...], hb, NT_DIMS, preferred_element_type=F32)
    aff = jax.nn.sigmoid(logits_t[:n_exp])
    sel = aff + rb_ref[...]
    srow = [sel[e:e + 1] for e in range(n_exp)]
    arow = [aff[e:e + 1] for e in range(n_exp)]

    gscore = []
    for g in range(N_GROUPS):
        a0, a1, a2, a3 = srow[g * epg:(g + 1) * epg]
        hi1, lo1 = jnp.maximum(a0, a1), jnp.minimum(a0, a1)
        hi2, lo2 = jnp.maximum(a2, a3), jnp.minimum(a2, a3)
        gscore.append(jnp.maximum(hi1, hi2) + jnp.maximum(jnp.minimum(hi1, hi2), jnp.maximum(lo1, lo2)))
    grp = jnp.zeros((1, t), I32)
    best = gscore[0]
    for g in range(1, N_GROUPS):
        take = gscore[g] > best
        grp = jnp.where(take, g, grp)
        best = jnp.where(take, gscore[g], best)

    def pick(rows_, idx, n):
        out = rows_[0]
        for k in range(1, n):
            out = jnp.where(idx == k, rows_[k], out)
        return out

    in_s = [pick([srow[g * epg + k] for g in range(N_GROUPS)], grp, N_GROUPS) for k in range(epg)]
    in_a = [pick([arow[g * epg + k] for g in range(N_GROUPS)], grp, N_GROUPS) for k in range(epg)]

    i1 = jnp.zeros((1, t), I32)
    v1 = in_s[0]
    for k in range(1, epg):
        take = in_s[k] > v1
        i1 = jnp.where(take, k, i1)
        v1 = jnp.where(take, in_s[k], v1)
    i2 = jnp.where(i1 == 0, 1, 0)
    v2 = jnp.where(i1 == 0, in_s[1], in_s[0])
    for k in range(1, epg):
        take = jnp.logical_and(i1 != k, jnp.logical_and(i2 != k, in_s[k] > v2))
        i2 = jnp.where(take, k, i2)
        v2 = jnp.where(take, in_s[k], v2)
    e1 = grp * epg + i1
    e2 = grp * epg + i2
    a1 = pick(in_a, i1, epg)
    a2 = pick(in_a, i2, epg)
    denom = a1 + a2
    w1 = a1 / denom
    w2 = a2 / denom

    eiota = lax.broadcasted_iota(I32, (n_exp, t), 0)
    oh1 = jnp.where(eiota == e1, 1.0, 0.0)
    oh2 = jnp.where(eiota == e2, 1.0, 0.0)
    oh = oh1 + oh2
    before = lax.broadcasted_iota(I32, (t, t), 0) < lax.broadcasted_iota(I32, (t, t), 1)
    excl = jnp.dot(oh.astype(BF16), jnp.where(before, 1.0, 0.0).astype(BF16), preferred_element_type=F32)
    rank = base_sc[:, 0:1] + excl
    r1 = jnp.sum(oh1 * rank, axis=0, keepdims=True).astype(I32)
    r2 = jnp.sum(oh2 * rank, axis=0, keepdims=True).astype(I32)
    base_sc[...] = base_sc[...] + jnp.sum(oh, axis=1, keepdims=True)

    meta_ref[...] = jnp.concatenate([e1, e2, r1, r2, jnp.zeros((4, t), I32)], axis=0)
    wrows = jnp.concatenate([w1, w2, jnp.zeros((LANES - 2, t), F32)], axis=0)
    wcol_ref[...] = wrows.T
    cnt_ref[...] = base_sc[...].astype(I32)


def _router(x, g, mod, sc_chunk, sh_chunk, rw_pad, router_b, tt=256):
    b, s, d = x.shape
    n_exp = router_b.shape[0]
    assert n_exp // N_GROUPS == 4 and TOP_K == 2
    tt = min(tt, s)
    nt = s // tt
    n = b * s
    outs = (
        jax.ShapeDtypeStruct((b, s, d // 2), U32),
        jax.ShapeDtypeStruct((8, n), I32),
        jax.ShapeDtypeStruct((n, LANES), F32),
        jax.ShapeDtypeStruct((n_exp, LANES), I32),
    )
    return pl.pallas_call(
        functools.partial(_router_kernel, n_exp=n_exp),
        grid=(b, nt),
        in_specs=[
            pl.BlockSpec((1, tt, d), lambda bi, i: (bi, i, 0)),
            pl.BlockSpec((1, d), lambda bi, i: (0, 0)),
            _mod_spec(d, sc_chunk),
            _mod_spec(d, sh_chunk),
            pl.BlockSpec((LANES, d), lambda bi, i: (0, 0)),
            pl.BlockSpec((n_exp, 1), lambda bi, i: (0, 0)),
        ],
        out_specs=(
            pl.BlockSpec((1, tt, d // 2), lambda bi, i: (bi, i, 0)),
            pl.BlockSpec((8, tt), lambda bi, i: (0, bi * nt + i)),
            pl.BlockSpec((tt, LANES), lambda bi, i: (bi * nt + i, 0)),
            pl.BlockSpec((n_exp, LANES), lambda bi, i: (0, 0)),
        ),
        out_shape=outs,
        scratch_shapes=[pltpu.VMEM((n_exp, LANES), F32)],
        compiler_params=_params(("arbitrary", "arbitrary")),
        name="router",
    )(x, g.reshape(1, d), mod, mod, rw_pad, router_b.reshape(n_exp, 1))


def _row(ref, r):
    return ref.at[pl.ds(r, 1), :]


def _dispatch_kernel(dest_ref, fill_lo_ref, fill_hi_ref, h_ref, buf_hbm, zero_sc, sem, *, n_tok, n_exp):
    chunk = h_ref.shape[0]
    ci = pl.program_id(0)

    @pl.when(ci == 0)
    def _():
        zero_sc[...] = jnp.zeros_like(zero_sc)
        for e in range(n_exp + 1):
            lo, hi = fill_lo_ref[e], fill_hi_ref[e]

            def zfill(r, _):
                pltpu.make_async_copy(_row(zero_sc, 0), _row(buf_hbm, r), sem).start()
                return 0
            lax.fori_loop(lo, hi, zfill, 0)

            def zdrain(r, _):
                pltpu.make_async_copy(_row(zero_sc, 0), _row(buf_hbm, 0), sem).wait()
                return 0
            lax.fori_loop(lo, hi, zdrain, 0)

    def issue(t, _):
        for k in range(TOP_K):
            a = k * n_tok + ci * chunk + t
            pltpu.make_async_copy(_row(h_ref, t), _row(buf_hbm, dest_ref[a]), sem).start()
        return 0
    lax.fori_loop(0, chunk, issue, 0, unroll=DMA_ISSUE_UNROLL)

    def drain(t, _):
        for k in range(TOP_K):
            pltpu.make_async_copy(_row(h_ref, 0), _row(buf_hbm, 0), sem).wait()
        return 0
    lax.fori_loop(0, chunk, drain, 0)


def _dispatch(h_rows, dest, fill_lo, fill_hi, n_rows, chunk=256):
    n, w = h_rows.shape
    n_exp = fill_lo.shape[0] - 1
    chunk = min(chunk, n)
    return pl.pallas_call(
        functools.partial(_dispatch_kernel, n_tok=n, n_exp=n_exp),
        grid_spec=pltpu.PrefetchScalarGridSpec(
            num_scalar_prefetch=3,
            grid=(n // chunk,),
            in_specs=[pl.BlockSpec((chunk, w), lambda i, *_: (i, 0))],
            out_specs=pl.BlockSpec(memory_space=pl.ANY),
            scratch_shapes=[pltpu.VMEM((8, w), h_rows.dtype), pltpu.SemaphoreType.DMA(())],
        ),
        out_shape=jax.ShapeDtypeStruct((n_rows, w), h_rows.dtype),
        compiler_params=_params(("arbitrary",)),
        name="dispatch",
    )(dest, fill_lo, fill_hi, h_rows)


def _expert_kernel(blk_e_ref, nused_ref, x_ref, wg_hbm, wu_hbm, wd_hbm, o_ref,
                   wg_sc, wu_sc, wd_sc, stage_in, stage_out, sem, *, layer):
    bi = pl.program_id(0)
    e = blk_e_ref[bi]
    active = bi < nused_ref[0]
    new_expert = jnp.logical_or(bi == 0, e != blk_e_ref[jnp.maximum(bi - 1, 0)])

    @pl.when(jnp.logical_and(active, new_expert))
    def _():
        copies = []
        for src, dst, stage in ((wg_hbm, wg_sc, stage_in), (wu_hbm, wu_sc, stage_in), (wd_hbm, wd_sc, stage_out)):
            rows = stage.shape[1]
            for c in range(dst.shape[0] // rows):
                slot = len(copies) % 2
                cp = pltpu.make_async_copy(src.at[layer, e, pl.ds(c * rows, rows), :], stage.at[slot], sem.at[slot])
                copies.append((cp, stage, slot, dst, c * rows, rows))
        copies[0][0].start()
        for n, (cp, stage, slot, dst, r0, rows) in enumerate(copies):
            if n + 1 < len(copies):
                copies[n + 1][0].start()
            cp.wait()
            dst[r0:r0 + rows, :] = stage[slot].astype(BF16)

    @pl.when(active)
    def _():
        xs = _unpack_bf16_pairs(x_ref[...])
        gate = jnp.dot(xs, wg_sc[...], preferred_element_type=F32)
        up = jnp.dot(xs, wu_sc[...], preferred_element_type=F32)
        hid = (jax.nn.silu(gate) * up).astype(BF16)
        o_ref[...] = jnp.dot(hid, wd_sc[...], preferred_element_type=F32)

    @pl.when(jnp.logical_not(active))
    def _():
        o_ref[...] = jnp.zeros_like(o_ref)


def _experts(buf, blk_e, n_used, layer, w_gate, w_up, w_down):
    n_rows, half = buf.shape
    _, n_exp, d, f = w_gate.shape
    nb = n_rows // MOE_BLOCK
    stage_elems = 512 * 1024
    rows_in, rows_out = min(d, stage_elems // f), min(f, stage_elems // d)
    assert d % rows_in == 0 and f % rows_out == 0

    def row_map(bi, blk_e_ref, nused_ref):
        return (jnp.minimum(bi, nused_ref[0] - 1), 0)

    return pl.pallas_call(
        functools.partial(_expert_kernel, layer=layer),
        grid_spec=pltpu.PrefetchScalarGridSpec(
            num_scalar_prefetch=2,
            grid=(nb,),
            in_specs=[
                pl.BlockSpec((MOE_BLOCK, half), row_map),
                pl.BlockSpec(memory_space=pl.ANY),
                pl.BlockSpec(memory_space=pl.ANY),
                pl.BlockSpec(memory_space=pl.ANY),
            ],
            out_specs=pl.BlockSpec((MOE_BLOCK, d), lambda bi, *_: (bi, 0)),
            scratch_shapes=[
                pltpu.VMEM((d, f), BF16),
                pltpu.VMEM((d, f), BF16),
                pltpu.VMEM((f, d), BF16),
                pltpu.VMEM((2, rows_in, f), F32),
                pltpu.VMEM((2, rows_out, d), F32),
                pltpu.SemaphoreType.DMA((2,)),
            ],
        ),
        out_shape=jax.ShapeDtypeStruct((n_rows, d), F32),
        compiler_params=_params(("arbitrary",)),
        name="experts",
    )(blk_e, n_used, buf, w_gate, w_up, w_down)


def _combine_kernel(dest_ref, x_ref, g_ref, wcol_ref, ob_hbm, ng_ref, *rest, nt, n_tok, last):
    if last:
        o_ref, gbuf, sem = rest
    else:
        sc_ref, sh_ref, o_ref, h_ref, gbuf, sem = rest
    tt = x_ref.shape[1]
    n_steps = pl.num_programs(0) * nt
    step = pl.program_id(0) * nt + pl.program_id(1)
    slot = step % 2
    nxt = jnp.minimum(step + 1, n_steps - 1)

    def issue(step_idx, slot_idx, t):
        for k in range(TOP_K):
            a = k * n_tok + step_idx * tt + t
            pltpu.make_async_copy(_row(ob_hbm, dest_ref[a]), _row(gbuf.at[slot_idx, k], t),
                                  sem.at[slot_idx]).start()

    def drain(slot_idx):
        def body(t, _):
            for k in range(TOP_K):
                pltpu.make_async_copy(_row(ob_hbm, 0), _row(gbuf.at[0, 0], 0), sem.at[slot_idx]).wait()
            return 0
        lax.fori_loop(0, tt, body, 0)

    @pl.when(step == 0)
    def _():
        def first(t, _):
            issue(0, 0, t)
            return 0
        lax.fori_loop(0, tt, first, 0, unroll=DMA_ISSUE_UNROLL)

    drain(slot)

    def chunk(c, _):
        r0 = pl.multiple_of(c * COMBINE_ROWS, COMBINE_ROWS)
        for u in range(COMBINE_ROWS):
            issue(nxt, 1 - slot, r0 + u)
        rows = pl.ds(r0, COMBINE_ROWS)
        y = gbuf[slot, 0, rows, :] * wcol_ref[rows, 0:1] + gbuf[slot, 1, rows, :] * wcol_ref[rows, 1:2]
        x_new = x_ref[0, rows, :] + g_ref[0] * y
        if last:
            o_ref[0, rows, :] = _rms(x_new, ng_ref[...])
        else:
            o_ref[0, rows, :] = x_new
            h_ref[0, rows, :] = (_rms(x_new, ng_ref[...]) * (1.0 + sc_ref[0]) + sh_ref[0]).astype(h_ref.dtype)
        return 0
    lax.fori_loop(0, tt // COMBINE_ROWS, chunk, 0)

    @pl.when(step == n_steps - 1)
    def _():
        drain(1 - slot)


def _combine(x, mod, g_chunk, wcol, out_buf, dest, next_g, next_mod, tt=256):
    b, s, d = x.shape
    tt = min(tt, s)
    nt = s // tt
    last = next_mod is None
    row_spec = pl.BlockSpec((1, tt, d), lambda bi, i, *_: (bi, i, 0))
    in_specs = [
        row_spec,
        pl.BlockSpec((1, 1, d), lambda bi, i, *_: (bi, 0, g_chunk)),
        pl.BlockSpec((tt, LANES), lambda bi, i, *_: (bi * nt + i, 0)),
        pl.BlockSpec(memory_space=pl.ANY),
        pl.BlockSpec((1, d), lambda bi, i, *_: (0, 0)),
    ]
    args = [x, mod, wcol, out_buf, next_g.reshape(1, d)]
    if last:
        out_specs, out_shape = row_spec, jax.ShapeDtypeStruct((b, s, d), x.dtype)
    else:
        in_specs += [_mod_spec(d, 1), _mod_spec(d, 0)]
        args += [next_mod, next_mod]
        out_specs = (row_spec, row_spec)
        out_shape = (jax.ShapeDtypeStruct((b, s, d), x.dtype), jax.ShapeDtypeStruct((b, s, d), BF16))
    return pl.pallas_call(
        functools.partial(_combine_kernel, nt=nt, n_tok=b * s, last=last),
        grid_spec=pltpu.PrefetchScalarGridSpec(
            num_scalar_prefetch=1,
            grid=(b, nt),
            in_specs=in_specs,
            out_specs=out_specs,
            scratch_shapes=[pltpu.VMEM((2, TOP_K, tt, d), F32), pltpu.SemaphoreType.DMA((2,))],
        ),
        out_shape=out_shape,
        compiler_params=_params(("arbitrary", "arbitrary")),
        name="combine",
    )(dest, *args)


def _moe(x, g, mod, rw_pad, router_b, layer, w_gate, w_up, w_down, next_g, next_mod):
    b, s, d = x.shape
    n = b * s
    n_exp = router_b.shape[0]
    h_rows, meta, wcol, cnt = _router(x, g, mod, 4, 3, rw_pad, router_b)
    counts = cnt[:, 0]
    padded = (counts + MOE_BLOCK - 1) // MOE_BLOCK * MOE_BLOCK
    pad_end = jnp.cumsum(padded)
    start = pad_end - padded
    nb = (n * TOP_K + n_exp * (MOE_BLOCK - 1) + MOE_BLOCK - 1) // MOE_BLOCK
    blk_first = jnp.arange(nb, dtype=I32) * MOE_BLOCK
    blk_e = jnp.minimum(jnp.sum(blk_first[:, None] >= pad_end[None, :], axis=1), n_exp - 1).astype(I32)
    n_used = (pad_end[-1:] // MOE_BLOCK).astype(I32)
    is_e = meta[0:2, :, None] == jnp.arange(n_exp, dtype=I32)
    dest = (meta[2:4] + jnp.sum(jnp.where(is_e, start.astype(I32), 0), axis=-1)).reshape(-1)
    n_rows = nb * MOE_BLOCK
    fill_lo = jnp.concatenate([start + counts, pad_end[-1:]]).astype(I32)
    fill_hi = jnp.concatenate([pad_end, jnp.full((1,), n_rows, pad_end.dtype)]).astype(I32)
    buf = _dispatch(h_rows.reshape(n, d // 2), dest, fill_lo, fill_hi, n_rows)
    out_buf = _experts(buf, blk_e, n_used, layer, w_gate, w_up, w_down)
    return _combine(x, mod, 5, wcol, out_buf, dest, next_g, next_mod)


def _pad_cols(w, width):
    return jnp.pad(w, ((0, 0), (0, width - w.shape[1])))


def kernel(x, c, ada_w, ada_b, norm1, norm2, w_in_ab, fox_fb, kv_norm, w_uk, w_uv, w_out_ab, w_in_c, conv_w, w_out_c, rel_bias, router_w, router_b, exp_gate, exp_up, exp_down, final_norm):
    depth, d = norm1.shape
    b, s, _ = x.shape
    fox_heads = fox_fb.shape[1]
    dsa_heads, rank = w_uk.shape[1], w_uk.shape[2]
    fox_w, dsa_w = fox_heads * HEAD_DIM, dsa_heads * HEAD_DIM
    even_cols = w_in_ab.shape[2]
    idx_heads = (even_cols - 3 * fox_w - fox_heads - dsa_w - rank - IDX_DIM) // (IDX_DIM + 1)
    iq_w = idx_heads * IDX_DIM
    o_fa = 3 * fox_w
    o_qb = o_fa + fox_heads
    o_ckv = o_qb + dsa_w
    o_iq = o_ckv + rank
    o_ik = o_iq + iq_w
    o_iw = o_ik + IDX_DIM
    assert o_iw + idx_heads == even_cols and fox_heads <= LANES and idx_heads <= LANES

    mods = _ada_mod(c, ada_w, ada_b)
    rw_pad = _pad_cols(router_w, LANES).T.astype(BF16)
    experts = (exp_gate, exp_up, exp_down)
    mod_of = [mods[l].reshape(b, 1, 6 * d) for l in range(depth)]
    h = _norm_mod(x, norm1[0], mod_of[0], 1, 0)
    for l in range(depth):
        mod = mod_of[l]
        i = l // 2
        if l % 2 == 0:
            w = w_in_ab[i]
            w_a = jnp.concatenate([w[:, :o_fa], w[:, o_qb:o_ckv], w[:, o_iq:o_ik]], axis=1).astype(BF16)
            w_s = jnp.concatenate([w[:, o_ckv:o_iq], _pad_cols(w[:, o_fa:o_qb], LANES),
                                   _pad_cols(w[:, o_ik:o_iw], LANES), _pad_cols(w[:, o_iw:], LANES)],
                                  axis=1).astype(BF16)
            z_a = _matmul(h, w_a, BF16)
            z_s = _matmul(h, w_s, F32)
            ckv_n, neg_cum, ik_e, ik_o, iw_s = _even_prep(z_s, kv_norm[i], fox_fb[i], idx_heads)
            o_a = _fox_attention(z_a, neg_cum, fox_heads)
            o_b = _dsa_attention(z_a, iw_s, ckv_n, ik_e, ik_o, w_uk[i].astype(BF16), w_uv[i].astype(BF16),
                                 rel_bias, fox_w, idx_heads)
            w_o = w_out_ab[i].astype(BF16)
            x = _matmul_residual([(o_a, w_o[:fox_w]), (o_b, w_o[fox_w:])], x, mod, 2)
        else:
            y = _shortconv(h, w_in_c[i].astype(BF16), conv_w[i])
            x = _matmul_residual([(y, w_out_c[i].astype(BF16))], x, mod, 2)
        if l + 1 < depth:
            x, h = _moe(x, norm2[l], mod, rw_pad, router_b, l, *experts, norm1[l + 1], mod_of[l + 1])
        else:
            x = _moe(x, norm2[l], mod, rw_pad, router_b, l, *experts, final_norm, None)
    return x
```

```python
import functools
import math

import numpy as np
import jax
import jax.numpy as jnp
from jax import lax
from jax.experimental import pallas as pl
from jax.experimental.pallas import tpu as pltpu

F32 = jnp.float32
BF16 = jnp.bfloat16
I32 = jnp.int32
U32 = jnp.uint32

HEAD_DIM = 128
IDX_DIM = 64
TOPK_MAX = 256
N_BUCKETS = 32
MAX_DISTANCE = 128
N_GROUPS = 4
TOP_K = 2
EPS = 1e-6
LANES = 128
VMEM_LIMIT = 56 * 1024 * 1024
MOE_BLOCK = 256
EXPERT_STAGE_SLOTS = 3
COMBINE_ROWS = 64
DMA_ISSUE_UNROLL = 8
NT_DIMS = (((1,), (1,)), ((), ()))


def _params(sem, vmem=VMEM_LIMIT):
    return pltpu.CompilerParams(dimension_semantics=sem, vmem_limit_bytes=vmem)


def _ada_kernel(c_ref, w_ref, b_ref, o_ref):
    c = c_ref[...]
    ca = (c * jax.nn.sigmoid(c)).astype(BF16)
    o_ref[0] = jnp.dot(ca, w_ref[0].astype(BF16), preferred_element_type=F32) + b_ref[0]


def _ada_mod(c, ada_w, ada_b):
    depth, d, n6 = ada_w.shape
    b = c.shape[0]
    tn = min(512, n6)
    return pl.pallas_call(
        _ada_kernel,
        grid=(depth, n6 // tn),
        in_specs=[
            pl.BlockSpec((b, d), lambda l, j: (0, 0)),
            pl.BlockSpec((1, d, tn), lambda l, j: (l, 0, j)),
            pl.BlockSpec((1, 1, tn), lambda l, j: (l, 0, j)),
        ],
        out_specs=pl.BlockSpec((1, b, tn), lambda l, j: (l, 0, j)),
        out_shape=jax.ShapeDtypeStruct((depth, b, n6), F32),
        compiler_params=_params(("arbitrary", "arbitrary")),
        name="ada_mod",
    )(c, ada_w, ada_b.reshape(depth, 1, n6))


def _rms(x, g):
    return x * lax.rsqrt(jnp.mean(x * x, axis=-1, keepdims=True) + EPS) * g


def _norm_mod_kernel(x_ref, g_ref, sc_ref, sh_ref, o_ref):
    y = _rms(x_ref[0], g_ref[...])
    o_ref[0] = (y * (1.0 + sc_ref[0]) + sh_ref[0]).astype(o_ref.dtype)


def _mod_spec(d, chunk):
    return pl.BlockSpec((1, 1, d), lambda b, *_: (b, 0, chunk))


def _norm_mod(x, g, mod, sc_chunk, sh_chunk, ts=256):
    b, s, d = x.shape
    ts = min(ts, s)
    return pl.pallas_call(
        _norm_mod_kernel,
        grid=(b, s // ts),
        in_specs=[
            pl.BlockSpec((1, ts, d), lambda bi, i: (bi, i, 0)),
            pl.BlockSpec((1, d), lambda bi, i: (0, 0)),
            _mod_spec(d, sc_chunk),
            _mod_spec(d, sh_chunk),
        ],
        out_specs=pl.BlockSpec((1, ts, d), lambda bi, i: (bi, i, 0)),
        out_shape=jax.ShapeDtypeStruct((b, s, d), BF16),
        compiler_params=_params(("arbitrary", "arbitrary")),
        name="norm_mod",
    )(x, g.reshape(1, d), mod, mod)


def _mm_kernel(a_ref, w_ref, o_ref):
    o_ref[0] = jnp.dot(a_ref[0], w_ref[...], preferred_element_type=F32).astype(o_ref.dtype)


def _pick_tile(n, target):
    if n <= target:
        return n
    return max(t for t in range(LANES, target + 1, LANES) if n % t == 0)


def _matmul(a, w, out_dtype, tm=1024, tn=1024):
    b, s, k = a.shape
    n = w.shape[1]
    tm, tn = min(tm, s), _pick_tile(n, tn)
    return pl.pallas_call(
        _mm_kernel,
        grid=(b, s // tm, n // tn),
        in_specs=[
            pl.BlockSpec((1, tm, k), lambda bi, i, j: (bi, i, 0)),
            pl.BlockSpec((k, tn), lambda bi, i, j: (0, j)),
        ],
        out_specs=pl.BlockSpec((1, tm, tn), lambda bi, i, j: (bi, i, j)),
        out_shape=jax.ShapeDtypeStruct((b, s, n), out_dtype),
        compiler_params=_params(("arbitrary", "arbitrary", "arbitrary")),
        name="matmul",
    )(a, w)


def _mm_res_kernel(*refs, n_pairs):
    x_ref, g_ref, o_ref = refs[2 * n_pairs:]
    acc = None
    for p in range(n_pairs):
        part = jnp.dot(refs[2 * p][0], refs[2 * p + 1][...], preferred_element_type=F32)
        acc = part if acc is None else acc + part
    o_ref[0] = x_ref[0] + g_ref[0] * acc


def _matmul_residual(pairs, x, mod, g_chunk, tm=1024, tn=512):
    b, s, d = x.shape
    tm, tn = min(tm, s), min(tn, d)
    in_specs, args = [], []
    for a, w in pairs:
        k = a.shape[2]
        in_specs += [pl.BlockSpec((1, tm, k), lambda bi, i, j: (bi, i, 0)),
                     pl.BlockSpec((k, tn), lambda bi, i, j: (0, j))]
        args += [a, w]
    nj = d // tn
    in_specs += [pl.BlockSpec((1, tm, tn), lambda bi, i, j: (bi, i, j)),
                 pl.BlockSpec((1, 1, tn), lambda bi, i, j: (bi, 0, g_chunk * nj + j))]
    return pl.pallas_call(
        functools.partial(_mm_res_kernel, n_pairs=len(pairs)),
        grid=(b, s // tm, nj),
        in_specs=in_specs,
        out_specs=pl.BlockSpec((1, tm, tn), lambda bi, i, j: (bi, i, j)),
        out_shape=jax.ShapeDtypeStruct((b, s, d), x.dtype),
        compiler_params=_params(("arbitrary", "arbitrary", "arbitrary")),
        name="matmul_residual",
    )(*args, x, mod)


def _lane_cumsum(x):
    n = x.shape[-1]
    lane = lax.broadcasted_iota(I32, x.shape, x.ndim - 1)
    d = 1
    while d < n:
        x = x + jnp.where(lane >= d, pltpu.roll(x, d, x.ndim - 1), 0.0)
        d *= 2
    return x


def _even_prep_kernel(z_ref, kvn_ref, fb_ref, ckv_ref, negcum_ref, ike_ref, iko_ref, iw_ref,
                      *, rank, fox_heads, idx_heads):
    z = z_ref[0]
    ckv_ref[0] = _rms(z[:, :rank], kvn_ref[...]).astype(BF16)
    f_t = z[:, rank:rank + LANES].T
    logf = jax.nn.log_sigmoid(f_t[:fox_heads] + fb_ref[...])
    negcum_ref[0] = -_lane_cumsum(logf)
    ik = z[:, rank + LANES:rank + LANES + IDX_DIM].astype(BF16)
    zero = jnp.zeros_like(ik)
    ike_ref[0] = jnp.concatenate([ik, zero], axis=1)
    iko_ref[0] = jnp.concatenate([zero, ik], axis=1)
    iw_ref[0] = z[:, rank + 2 * LANES:rank + 2 * LANES + idx_heads] * (idx_heads ** -0.5) * (IDX_DIM ** -0.5)


def _even_prep(z_s, kv_norm, fox_fb, idx_heads):
    b, s, w = z_s.shape
    rank = kv_norm.shape[0]
    fh = fox_fb.shape[0]
    outs = (
        jax.ShapeDtypeStruct((b, s, rank), BF16),
        jax.ShapeDtypeStruct((b, fh, s), F32),
        jax.ShapeDtypeStruct((b, s, LANES), BF16),
        jax.ShapeDtypeStruct((b, s, LANES), BF16),
        jax.ShapeDtypeStruct((b, s, idx_heads), F32),
    )
    return pl.pallas_call(
        functools.partial(_even_prep_kernel, rank=rank, fox_heads=fh, idx_heads=idx_heads),
        grid=(b,),
        in_specs=[
            pl.BlockSpec((1, s, w), lambda bi: (bi, 0, 0)),
            pl.BlockSpec((1, rank), lambda bi: (0, 0)),
            pl.BlockSpec((fh, 1), lambda bi: (0, 0)),
        ],
        out_specs=(
            pl.BlockSpec((1, s, rank), lambda bi: (bi, 0, 0)),
            pl.BlockSpec((1, fh, s), lambda bi: (bi, 0, 0)),
            pl.BlockSpec((1, s, LANES), lambda bi: (bi, 0, 0)),
            pl.BlockSpec((1, s, LANES), lambda bi: (bi, 0, 0)),
            pl.BlockSpec((1, s, idx_heads), lambda bi: (bi, 0, 0)),
        ),
        out_shape=outs,
        compiler_params=_params(("arbitrary",)),
        name="even_prep",
    )(z_s, kv_norm.reshape(1, rank), fox_fb.reshape(fh, 1))


def _fox_kernel(q_ref, k_ref, v_ref, nc_ref, o_ref, *, heads, tq):
    s_len = q_ref.shape[1]
    scale = HEAD_DIM ** -0.5
    lower = lax.broadcasted_iota(I32, (tq, tq), 0) >= lax.broadcasted_iota(I32, (tq, tq), 1)
    head_cols = [slice(hh * HEAD_DIM, (hh + 1) * HEAD_DIM) for hh in range(heads)]

    def q_body(qi, _):
        q0 = pl.multiple_of(qi * tq, tq)
        qs = [q_ref[0, pl.ds(q0, tq), hs] for hs in head_cols]

        def k_step(kj, carries, diagonal):
            k0 = pl.multiple_of(kj * tq, tq)
            out = []
            for hh, hs in enumerate(head_cols):
                m, l, acc = carries[hh]
                k = k_ref[0, pl.ds(k0, tq), hs]
                v = v_ref[0, pl.ds(k0, tq), hs]
                s = lax.dot_general(qs[hh], k, NT_DIMS, preferred_element_type=F32) * scale
                s = s + nc_ref[0, hh, pl.ds(kj, 1), :]
                if diagonal:
                    s = jnp.where(lower, s, -jnp.inf)
                m_new = jnp.maximum(m, jnp.max(s, axis=-1, keepdims=True))
                alpha = jnp.exp(m - m_new)
                p = jnp.exp(s - m_new)
                l = alpha * l + jnp.sum(p, axis=-1, keepdims=True)
                acc = alpha * acc + jnp.dot(p.astype(BF16), v, preferred_element_type=F32)
                out.append((m_new, l, acc))
            return tuple(out)

        init = tuple((jnp.full((tq, 1), -jnp.inf, F32), jnp.zeros((tq, 1), F32),
                      jnp.zeros((tq, HEAD_DIM), F32)) for _ in head_cols)
        carries = lax.fori_loop(0, qi, functools.partial(k_step, diagonal=False), init)
        carries = k_step(qi, carries, diagonal=True)
        for (m, l, acc), hs in zip(carries, head_cols):
            o_ref[0, pl.ds(q0, tq), hs] = (acc / l).astype(o_ref.dtype)
        return 0

    lax.fori_loop(0, s_len // tq, q_body, 0)


def _fox_attention(z_a, neg_cum, fox_heads, heads_per_step=2, tq=512):
    b, s, _ = z_a.shape
    tq = tk = min(tq, s)
    hp = heads_per_step
    wblk = hp * HEAD_DIM
    nblk = fox_heads // hp
    nc = neg_cum.reshape(b, fox_heads, s // tk, tk)
    return pl.pallas_call(
        functools.partial(_fox_kernel, heads=hp, tq=tq),
        grid=(b, nblk),
        in_specs=[
            pl.BlockSpec((1, s, wblk), lambda bi, h: (bi, 0, h)),
            pl.BlockSpec((1, s, wblk), lambda bi, h: (bi, 0, nblk + h)),
            pl.BlockSpec((1, s, wblk), lambda bi, h: (bi, 0, 2 * nblk + h)),
            pl.BlockSpec((1, hp, s // tk, tk), lambda bi, h: (bi, h, 0, 0)),
        ],
        out_specs=pl.BlockSpec((1, s, wblk), lambda bi, h: (bi, 0, h)),
        out_shape=jax.ShapeDtypeStruct((b, s, fox_heads * HEAD_DIM), BF16),
        compiler_params=_params(("arbitrary", "arbitrary")),
        name="fox_attention",
    )(z_a, z_a, z_a, nc)


DSA_TQ = 256
INT_MIN = int(np.iinfo(np.int32).min)


def _t5_thresholds(max_d):
    max_exact = N_BUCKETS // 2
    d = np.arange(max_d)
    df = np.maximum(d, 1).astype(np.float32)
    large = max_exact + (np.log(df / np.float32(max_exact)) / np.float32(math.log(MAX_DISTANCE / max_exact))
                         * np.float32(N_BUCKETS - max_exact)).astype(np.int32)
    table = np.where(d < max_exact, d, np.minimum(large, N_BUCKETS - 1))
    assert np.all(np.diff(table) >= 0) and table[-1] == N_BUCKETS - 1
    return [int(np.argmax(table >= b)) for b in range(N_BUCKETS)]


def _bias_tile_kernel(rb_ref, o_ref, *, tq, thresholds):
    h = pl.program_id(0)
    a = lax.broadcasted_iota(I32, (tq, 2 * tq), 0)
    c = lax.broadcasted_iota(I32, (tq, 2 * tq), 1)
    d = tq + a - c
    val = jnp.full((tq, 2 * tq), rb_ref[0, h], F32)
    for b in range(1, N_BUCKETS):
        val = jnp.where(d >= thresholds[b], rb_ref[b, h], val)
    o_ref[0] = val


def _bias_tiles(rel_bias, tq):
    heads = rel_bias.shape[1]
    thresholds = _t5_thresholds(2 * tq)
    assert thresholds[N_BUCKETS - 1] <= tq + 1
    return pl.pallas_call(
        functools.partial(_bias_tile_kernel, tq=tq, thresholds=thresholds),
        grid=(heads,),
        in_specs=[pl.BlockSpec(memory_space=pltpu.SMEM)],
        out_specs=pl.BlockSpec((1, tq, 2 * tq), lambda h: (h, 0, 0)),
        out_shape=jax.ShapeDtypeStruct((heads, tq, 2 * tq), F32),
        compiler_params=_params(("arbitrary",)),
        name="bias_tiles",
    )(rel_bias)


def _dsa_kernel(rb_ref, qb_ref, iq_ref, iw_ref, ckv_ref, ike_ref, iko_ref, wuk_ref, wuv_ref, tile_ref,
                o_ref, key_sc, madd_sc, p_sc, *, heads, idx_heads, topk, tq, group, i):
    s_len, rank = ckv_ref.shape[1], ckv_ref.shape[2]
    scale = HEAD_DIM ** -0.5

    iq = iq_ref[0]
    iw = iw_ref[0]
    ike = ike_ref[0]
    iko = iko_ref[0]
    nch = idx_heads * IDX_DIM // LANES
    score = jnp.zeros((tq, s_len), F32)
    for g0 in range(0, nch, 2):
        cs = list(range(g0, min(g0 + 2, nch)))
        a = jnp.concatenate([iq[:, c * LANES:(c + 1) * LANES] for c in cs], axis=0)
        r_e = lax.dot_general(a, ike, NT_DIMS, preferred_element_type=F32)
        r_o = lax.dot_general(a, iko, NT_DIMS, preferred_element_type=F32)
        for n, c in enumerate(cs):
            rows = slice(n * tq, (n + 1) * tq)
            score = score + jnp.maximum(r_e[rows], 0.0) * iw[:, 2 * c:2 * c + 1]
            score = score + jnp.maximum(r_o[rows], 0.0) * iw[:, 2 * c + 1:2 * c + 2]

    t_idx = i * tq + lax.broadcasted_iota(I32, (tq, s_len), 0)
    s_idx = lax.broadcasted_iota(I32, (tq, s_len), 1)
    bits = pltpu.bitcast(score, I32)
    key = bits ^ ((bits >> 31) & 0x7FFFFFFF)
    key_sc[...] = jnp.where(t_idx >= s_idx, key, INT_MIN)

    def bit_step(it, lo):
        cand = lo + lax.shift_left(jnp.int32(1), 31 - it)
        cnt = jnp.sum(jnp.where(key_sc[...] >= cand, 1.0, 0.0), axis=-1, keepdims=True)
        return jnp.where(cnt >= topk, cand, lo)
    thr = lax.fori_loop(0, 32, bit_step, jnp.full((tq, 1), INT_MIN, I32))

    n_gt = jnp.sum(jnp.where(key_sc[...] > thr, 1.0, 0.0), axis=-1, keepdims=True)
    n_eq = jnp.sum(jnp.where(key_sc[...] == thr, 1.0, 0.0), axis=-1, keepdims=True)
    need = topk - n_gt
    nbits = (s_len - 1).bit_length()

    def idx_step(it, p):
        cand = p + lax.shift_left(jnp.int32(1), nbits - 1 - it)
        tied_below = jnp.where(key_sc[...] == thr, jnp.where(s_idx < cand, 1.0, 0.0), 0.0)
        cnt = jnp.sum(tied_below, axis=-1, keepdims=True)
        return jnp.where(cnt < need, cand, p)

    excess = jnp.where(jnp.logical_and(n_eq > need, thr != INT_MIN), 1, 0)
    p_last = lax.cond(jnp.max(excess) > 0,
                      lambda: lax.fori_loop(0, nbits, idx_step, jnp.zeros((tq, 1), I32)),
                      lambda: jnp.full((tq, 1), s_len, I32))

    key = key_sc[...]
    keep = jnp.where(key > thr, 0.0, jnp.where(key == thr, jnp.where(s_idx <= p_last, 0.0, -jnp.inf), -jnp.inf))
    madd_sc[...] = jnp.where(t_idx >= s_idx, keep, -jnp.inf)

    qb = qb_ref[0]
    ckv = ckv_ref[0]
    madd = madd_sc[...]
    n_far = max(s_len - 2 * tq, 0)
    for g in range(heads // group):
        hs = range(g * group, (g + 1) * group)
        qg = jnp.concatenate(
            [lax.dot_general(qb[:, h * HEAD_DIM:(h + 1) * HEAD_DIM], wuk_ref[h], NT_DIMS,
                             preferred_element_type=F32).astype(BF16) for h in hs], axis=0)
        sg = lax.dot_general(qg, ckv, NT_DIMS, preferred_element_type=F32) * scale
        sums = []
        for n, h in enumerate(hs):
            near = tile_ref[h][:, 2 * tq - (s_len - n_far):]
            if n_far:
                far = jnp.full((tq, n_far), rb_ref[N_BUCKETS - 1, h], F32)
                near = jnp.concatenate([far, near], axis=1)
            s = sg[n * tq:(n + 1) * tq] + near + madd
            m = jnp.max(s, axis=-1, keepdims=True)
            p = jnp.exp(s - m)
            sums.append(jnp.sum(p, axis=-1, keepdims=True))
            p_sc[n] = p.astype(BF16)
        og = jnp.dot(p_sc[...].reshape(group * tq, s_len), ckv, preferred_element_type=F32)
        for n, h in enumerate(hs):
            ol = (og[n * tq:(n + 1) * tq] / sums[n]).astype(BF16)
            o_ref[0, :, h * HEAD_DIM:(h + 1) * HEAD_DIM] = jnp.dot(
                ol, wuv_ref[h], preferred_element_type=F32).astype(o_ref.dtype)


def _dsa_attention(z_a, iw_s, ckv_n, ik_e, ik_o, w_uk, w_uv, rel_bias, fox_w, idx_heads, group=2):
    b, s, _ = z_a.shape
    heads, rank, _ = w_uk.shape
    tq = min(DSA_TQ, s)
    group = min(group, heads)
    assert s & (s - 1) == 0 and heads % group == 0
    dsa_w = heads * HEAD_DIM
    iq_w = idx_heads * IDX_DIM
    assert (3 * fox_w) % dsa_w == 0 and (3 * fox_w + dsa_w) % iq_w == 0
    qb_blk = 3 * fox_w // dsa_w
    iq_blk = (3 * fox_w + dsa_w) // iq_w
    topk = min(TOPK_MAX, s // 4)
    tiles = _bias_tiles(rel_bias, tq)
    outs = []
    for i in range(s // tq):
        sk = (i + 1) * tq
        const = dict(pipeline_mode=pl.Buffered(1))
        outs.append(pl.pallas_call(
            functools.partial(_dsa_kernel, heads=heads, idx_heads=idx_heads, topk=topk, tq=tq, group=group, i=i),
            grid=(b,),
            in_specs=[
                pl.BlockSpec(memory_space=pltpu.SMEM),
                pl.BlockSpec((1, tq, dsa_w), lambda bi, i=i: (bi, i, qb_blk)),
                pl.BlockSpec((1, tq, iq_w), lambda bi, i=i: (bi, i, iq_blk)),
                pl.BlockSpec((1, tq, idx_heads), lambda bi, i=i: (bi, i, 0)),
                pl.BlockSpec((1, sk, rank), lambda bi: (bi, 0, 0)),
                pl.BlockSpec((1, sk, LANES), lambda bi: (bi, 0, 0)),
                pl.BlockSpec((1, sk, LANES), lambda bi: (bi, 0, 0)),
                pl.BlockSpec((heads, rank, HEAD_DIM), lambda bi: (0, 0, 0), **const),
                pl.BlockSpec((heads, rank, HEAD_DIM), lambda bi: (0, 0, 0), **const),
                pl.BlockSpec((heads, tq, 2 * tq), lambda bi: (0, 0, 0), **const),
            ],
            out_specs=pl.BlockSpec((1, tq, dsa_w), lambda bi: (bi, 0, 0)),
            out_shape=jax.ShapeDtypeStruct((b, tq, dsa_w), BF16),
            scratch_shapes=[
                pltpu.VMEM((tq, sk), I32),
                pltpu.VMEM((tq, sk), F32),
                pltpu.VMEM((group, tq, sk), BF16),
            ],
            compiler_params=_params(("arbitrary",)),
            name=f"dsa_attention_{i}",
        )(rel_bias, z_a, z_a, iw_s, ckv_n, ik_e, ik_o, w_uk, w_uv, tiles))
    return jnp.concatenate(outs, axis=1)


def _shortconv_kernel(a_ref, wb_ref, wc_ref, wu_ref, cw_ref, o_ref):
    a = a_ref[0]
    bg = jnp.dot(a, wb_ref[...], preferred_element_type=F32)
    cg = jnp.dot(a, wc_ref[...], preferred_element_type=F32)
    u = jnp.dot(a, wu_ref[...], preferred_element_type=F32)
    p = cg * u
    row = lax.broadcasted_iota(I32, p.shape, 0)
    p1 = jnp.where(row >= 1, pltpu.roll(p, 1, 0), 0.0)
    p2 = jnp.where(row >= 2, pltpu.roll(p, 2, 0), 0.0)
    cw = cw_ref[...]
    y = p2 * cw[0:1] + p1 * cw[1:2] + p * cw[2:3]
    o_ref[0] = (bg * y).astype(o_ref.dtype)


def _shortconv(h, w_in, conv_w, tn=256):
    b, s, d = h.shape
    assert conv_w.shape[0] == 3
    tn = min(tn, d)
    nj = d // tn
    return pl.pallas_call(
        _shortconv_kernel,
        grid=(b, nj),
        in_specs=[
            pl.BlockSpec((1, s, d), lambda bi, j: (bi, 0, 0), pipeline_mode=pl.Buffered(1)),
            pl.BlockSpec((d, tn), lambda bi, j: (0, j)),
            pl.BlockSpec((d, tn), lambda bi, j: (0, nj + j)),
            pl.BlockSpec((d, tn), lambda bi, j: (0, 2 * nj + j)),
            pl.BlockSpec((3, tn), lambda bi, j: (0, j)),
        ],
        out_specs=pl.BlockSpec((1, s, tn), lambda bi, j: (bi, 0, j)),
        out_shape=jax.ShapeDtypeStruct((b, s, d), BF16),
        compiler_params=_params(("arbitrary", "arbitrary")),
        name="shortconv",
    )(h, w_in, w_in, w_in, conv_w)


def _pack_bf16_pairs(xb):
    half = xb.shape[1] // 2
    bits = pltpu.bitcast(xb.astype(F32), U32)
    return (bits[:, half:] & jnp.uint32(0xFFFF0000)) | (bits[:, :half] >> 16)


def _unpack_bf16_pairs(words):
    lo = pltpu.bitcast(words << 16, F32).astype(BF16)
    hi = pltpu.bitcast(words & jnp.uint32(0xFFFF0000), F32).astype(BF16)
    return jnp.concatenate([lo, hi], axis=1)


def _router_kernel(x_ref, g_ref, sc_ref, sh_ref, rw_ref, rb_ref, h_ref, meta_ref, wcol_ref, cnt_ref,
                   base_sc, *, n_exp):
    epg = n_exp // N_GROUPS
    first = jnp.logical_and(pl.program_id(0) == 0, pl.program_id(1) == 0)

    @pl.when(first)
    def _():
        base_sc[...] = jnp.zeros_like(base_sc)

    h = _rms(x_ref[0], g_ref[...]) * (1.0 + sc_ref[0]) + sh_ref[0]
    t, d = h.shape
    hb = h.astype(BF16)
    h_ref[0] = _pack_bf16_pairs(hb)

    logits_t = lax.dot_general(rw_ref[...], hb, NT_DIMS, preferred_element_type=F32)
    aff = jax.nn.sigmoid(logits_t[:n_exp])
    sel = aff + rb_ref[...]
    srow = [sel[e:e + 1] for e in range(n_exp)]
    arow = [aff[e:e + 1] for e in range(n_exp)]

    gscore = []
    for g in range(N_GROUPS):
        a0, a1, a2, a3 = srow[g * epg:(g + 1) * epg]
        hi1, lo1 = jnp.maximum(a0, a1), jnp.minimum(a0, a1)
        hi2, lo2 = jnp.maximum(a2, a3), jnp.minimum(a2, a3)
        gscore.append(jnp.maximum(hi1, hi2) + jnp.maximum(jnp.minimum(hi1, hi2), jnp.maximum(lo1, lo2)))
    grp = jnp.zeros((1, t), I32)
    best = gscore[0]
    for g in range(1, N_GROUPS):
        take = gscore[g] > best
        grp = jnp.where(take, g, grp)
        best = jnp.where(take, gscore[g], best)

    def pick(rows_, idx, n):
        out = rows_[0]
        for k in range(1, n):
            out = jnp.where(idx == k, rows_[k], out)
        return out

    in_s = [pick([srow[g * epg + k] for g in range(N_GROUPS)], grp, N_GROUPS) for k in range(epg)]
    in_a = [pick([arow[g * epg + k] for g in range(N_GROUPS)], grp, N_GROUPS) for k in range(epg)]

    i1 = jnp.zeros((1, t), I32)
    v1 = in_s[0]
    for k in range(1, epg):
        take = in_s[k] > v1
        i1 = jnp.where(take, k, i1)
        v1 = jnp.where(take, in_s[k], v1)
    i2 = jnp.where(i1 == 0, 1, 0)
    v2 = jnp.where(i1 == 0, in_s[1], in_s[0])
    for k in range(1, epg):
        take = jnp.logical_and(i1 != k, jnp.logical_and(i2 != k, in_s[k] > v2))
        i2 = jnp.where(take, k, i2)
        v2 = jnp.where(take, in_s[k], v2)
    e1 = grp * epg + i1
    e2 = grp * epg + i2
    a1 = pick(in_a, i1, epg)
    a2 = pick(in_a, i2, epg)
    denom = a1 + a2
    w1 = a1 / denom
    w2 = a2 / denom

    eiota = lax.broadcasted_iota(I32, (n_exp, t), 0)
    oh1 = jnp.where(eiota == e1, 1.0, 0.0)
    oh2 = jnp.where(eiota == e2, 1.0, 0.0)
    oh = oh1 + oh2
    before = lax.broadcasted_iota(I32, (t, t), 0) < lax.broadcasted_iota(I32, (t, t), 1)
    excl = jnp.dot(oh.astype(BF16), jnp.where(before, 1.0, 0.0).astype(BF16), preferred_element_type=F32)
    rank = base_sc[:, 0:1] + excl
    r1 = jnp.sum(oh1 * rank, axis=0, keepdims=True).astype(I32)
    r2 = jnp.sum(oh2 * rank, axis=0, keepdims=True).astype(I32)
    base_sc[...] = base_sc[...] + jnp.sum(oh, axis=1, keepdims=True)

    meta_ref[...] = jnp.concatenate([e1, e2, r1, r2, jnp.zeros((4, t), I32)], axis=0)
    wrows = jnp.concatenate([w1, w2, jnp.zeros((LANES - 2, t), F32)], axis=0)
    wcol_ref[...] = wrows.T
    cnt_ref[...] = base_sc[...].astype(I32)


def _router(x, g, mod, sc_chunk, sh_chunk, rw_pad, router_b, tt=256):
    b, s, d = x.shape
    n_exp = router_b.shape[0]
    assert n_exp // N_GROUPS == 4 and TOP_K == 2
    tt = min(tt, s)
    nt = s // tt
    n = b * s
    outs = (
        jax.ShapeDtypeStruct((b, s, d // 2), U32),
        jax.ShapeDtypeStruct((8, n), I32),
        jax.ShapeDtypeStruct((n, LANES), F32),
        jax.ShapeDtypeStruct((n_exp, LANES), I32),
    )
    return pl.pallas_call(
        functools.partial(_router_kernel, n_exp=n_exp),
        grid=(b, nt),
        in_specs=[
            pl.BlockSpec((1, tt, d), lambda bi, i: (bi, i, 0)),
            pl.BlockSpec((1, d), lambda bi, i: (0, 0)),
            _mod_spec(d, sc_chunk),
            _mod_spec(d, sh_chunk),
            pl.BlockSpec((LANES, d), lambda bi, i: (0, 0)),
            pl.BlockSpec((n_exp, 1), lambda bi, i: (0, 0)),
        ],
        out_specs=(
            pl.BlockSpec((1, tt, d // 2), lambda bi, i: (bi, i, 0)),
            pl.BlockSpec((8, tt), lambda bi, i: (0, bi * nt + i)),
            pl.BlockSpec((tt, LANES), lambda bi, i: (bi * nt + i, 0)),
            pl.BlockSpec((n_exp, LANES), lambda bi, i: (0, 0)),
        ),
        out_shape=outs,
        scratch_shapes=[pltpu.VMEM((n_exp, LANES), F32)],
        compiler_params=_params(("arbitrary", "arbitrary")),
        name="router",
    )(x, g.reshape(1, d), mod, mod, rw_pad, router_b.reshape(n_exp, 1))


def _row(ref, r):
    return ref.at[pl.ds(r, 1), :]


def _dispatch_kernel(dest_ref, fill_lo_ref, fill_hi_ref, h_ref, buf_hbm, zero_sc, sem, *, n_tok, n_exp):
    chunk = h_ref.shape[0]
    ci = pl.program_id(0)

    @pl.when(ci == 0)
    def _():
        zero_sc[...] = jnp.zeros_like(zero_sc)
        for e in range(n_exp + 1):
            lo, hi = fill_lo_ref[e], fill_hi_ref[e]

            def zfill(r, _):
                pltpu.make_async_copy(_row(zero_sc, 0), _row(buf_hbm, r), sem).start()
                return 0
            lax.fori_loop(lo, hi, zfill, 0)

            def zdrain(r, _):
                pltpu.make_async_copy(_row(zero_sc, 0), _row(buf_hbm, 0), sem).wait()
                return 0
            lax.fori_loop(lo, hi, zdrain, 0)

    def issue(t, _):
        for k in range(TOP_K):
            a = k * n_tok + ci * chunk + t
            pltpu.make_async_copy(_row(h_ref, t), _row(buf_hbm, dest_ref[a]), sem).start()
        return 0
    lax.fori_loop(0, chunk, issue, 0, unroll=DMA_ISSUE_UNROLL)

    def drain(t, _):
        for k in range(TOP_K):
            pltpu.make_async_copy(_row(h_ref, 0), _row(buf_hbm, 0), sem).wait()
        return 0
    lax.fori_loop(0, chunk, drain, 0)


def _dispatch(h_rows, dest, fill_lo, fill_hi, n_rows, chunk=256):
    n, w = h_rows.shape
    n_exp = fill_lo.shape[0] - 1
    chunk = min(chunk, n)
    return pl.pallas_call(
        functools.partial(_dispatch_kernel, n_tok=n, n_exp=n_exp),
        grid_spec=pltpu.PrefetchScalarGridSpec(
            num_scalar_prefetch=3,
            grid=(n // chunk,),
            in_specs=[pl.BlockSpec((chunk, w), lambda i, *_: (i, 0))],
            out_specs=pl.BlockSpec(memory_space=pl.ANY),
            scratch_shapes=[pltpu.VMEM((8, w), h_rows.dtype), pltpu.SemaphoreType.DMA(())],
        ),
        out_shape=jax.ShapeDtypeStruct((n_rows, w), h_rows.dtype),
        compiler_params=_params(("arbitrary",)),
        name="dispatch",
    )(dest, fill_lo, fill_hi, h_rows)


def _expert_kernel(blk_e_ref, nused_ref, x_ref, wg_hbm, wu_hbm, wd_hbm, o_ref,
                   wg_sc, wu_sc, wd_sc, stage_in, stage_out, sem, *, layer):
    bi = pl.program_id(0)
    e = blk_e_ref[bi]
    active = bi < nused_ref[0]
    new_expert = jnp.logical_or(bi == 0, e != blk_e_ref[jnp.maximum(bi - 1, 0)])

    @pl.when(jnp.logical_and(active, new_expert))
    def _():
        n_slots = stage_in.shape[0]
        copies = []
        for src, dst, stage in ((wg_hbm, wg_sc, stage_in), (wu_hbm, wu_sc, stage_in), (wd_hbm, wd_sc, stage_out)):
            rows = stage.shape[1]
            for c in range(dst.shape[0] // rows):
                slot = len(copies) % n_slots
                cp = pltpu.make_async_copy(src.at[layer, e, pl.ds(c * rows, rows), :], stage.at[slot], sem.at[slot])
                copies.append((cp, stage, slot, dst, c * rows, rows))
        for cp, *_ in copies[:n_slots - 1]:
            cp.start()
        for n, (cp, stage, slot, dst, r0, rows) in enumerate(copies):
            if n + n_slots - 1 < len(copies):
                copies[n + n_slots - 1][0].start()
            cp.wait()
            dst[r0:r0 + rows, :] = stage[slot].astype(BF16)

    @pl.when(active)
    def _():
        xs = _unpack_bf16_pairs(x_ref[...])
        gate = jnp.dot(xs, wg_sc[...], preferred_element_type=F32)
        up = jnp.dot(xs, wu_sc[...], preferred_element_type=F32)
        hid = (jax.nn.silu(gate) * up).astype(BF16)
        o_ref[...] = jnp.dot(hid, wd_sc[...], preferred_element_type=F32)

    @pl.when(jnp.logical_not(active))
    def _():
        o_ref[...] = jnp.zeros_like(o_ref)


def _experts(buf, blk_e, n_used, layer, w_gate, w_up, w_down):
    n_rows, half = buf.shape
    _, n_exp, d, f = w_gate.shape
    nb = n_rows // MOE_BLOCK
    stage_elems = 512 * 1024
    rows_in, rows_out = min(d, stage_elems // f), min(f, stage_elems // d)
    assert d % rows_in == 0 and f % rows_out == 0

    def row_map(bi, blk_e_ref, nused_ref):
        return (jnp.minimum(bi, nused_ref[0] - 1), 0)

    return pl.pallas_call(
        functools.partial(_expert_kernel, layer=layer),
        grid_spec=pltpu.PrefetchScalarGridSpec(
            num_scalar_prefetch=2,
            grid=(nb,),
            in_specs=[
                pl.BlockSpec((MOE_BLOCK, half), row_map),
                pl.BlockSpec(memory_space=pl.ANY),
                pl.BlockSpec(memory_space=pl.ANY),
                pl.BlockSpec(memory_space=pl.ANY),
            ],
            out_specs=pl.BlockSpec((MOE_BLOCK, d), lambda bi, *_: (bi, 0)),
            scratch_shapes=[
                pltpu.VMEM((d, f), BF16),
                pltpu.VMEM((d, f), BF16),
                pltpu.VMEM((f, d), BF16),
                pltpu.VMEM((EXPERT_STAGE_SLOTS, rows_in, f), F32),
                pltpu.VMEM((EXPERT_STAGE_SLOTS, rows_out, d), F32),
                pltpu.SemaphoreType.DMA((EXPERT_STAGE_SLOTS,)),
            ],
        ),
        out_shape=jax.ShapeDtypeStruct((n_rows, d), F32),
        compiler_params=_params(("arbitrary",)),
        name="experts",
    )(blk_e, n_used, buf, w_gate, w_up, w_down)


def _combine_kernel(dest_ref, x_ref, g_ref, wcol_ref, ob_hbm, ng_ref, *rest, nt, n_tok, last):
    if last:
        o_ref, gbuf, sem = rest
    else:
        sc_ref, sh_ref, o_ref, h_ref, gbuf, sem = rest
    tt = x_ref.shape[1]
    n_steps = pl.num_programs(0) * nt
    step = pl.program_id(0) * nt + pl.program_id(1)
    slot = step % 2
    nxt = jnp.minimum(step + 1, n_steps - 1)

    def issue(step_idx, slot_idx, t):
        for k in range(TOP_K):
            a = k * n_tok + step_idx * tt + t
            pltpu.make_async_copy(_row(ob_hbm, dest_ref[a]), _row(gbuf.at[slot_idx, k], t),
                                  sem.at[slot_idx]).start()

    def drain(slot_idx):
        def body(t, _):
            for k in range(TOP_K):
                pltpu.make_async_copy(_row(ob_hbm, 0), _row(gbuf.at[0, 0], 0), sem.at[slot_idx]).wait()
            return 0
        lax.fori_loop(0, tt, body, 0)

    @pl.when(step == 0)
    def _():
        def first(t, _):
            issue(0, 0, t)
            return 0
        lax.fori_loop(0, tt, first, 0, unroll=DMA_ISSUE_UNROLL)

    drain(slot)

    def chunk(c, _):
        r0 = pl.multiple_of(c * COMBINE_ROWS, COMBINE_ROWS)
        for u in range(COMBINE_ROWS):
            issue(nxt, 1 - slot, r0 + u)
        rows = pl.ds(r0, COMBINE_ROWS)
        y = gbuf[slot, 0, rows, :] * wcol_ref[rows, 0:1] + gbuf[slot, 1, rows, :] * wcol_ref[rows, 1:2]
        x_new = x_ref[0, rows, :] + g_ref[0] * y
        if last:
            o_ref[0, rows, :] = _rms(x_new, ng_ref[...])
        else:
            o_ref[0, rows, :] = x_new
            h_ref[0, rows, :] = (_rms(x_new, ng_ref[...]) * (1.0 + sc_ref[0]) + sh_ref[0]).astype(h_ref.dtype)
        return 0
    lax.fori_loop(0, tt // COMBINE_ROWS, chunk, 0)

    @pl.when(step == n_steps - 1)
    def _():
        drain(1 - slot)


def _combine(x, mod, g_chunk, wcol, out_buf, dest, next_g, next_mod, tt=256):
    b, s, d = x.shape
    tt = min(tt, s)
    nt = s // tt
    last = next_mod is None
    row_spec = pl.BlockSpec((1, tt, d), lambda bi, i, *_: (bi, i, 0))
    in_specs = [
        row_spec,
        pl.BlockSpec((1, 1, d), lambda bi, i, *_: (bi, 0, g_chunk)),
        pl.BlockSpec((tt, LANES), lambda bi, i, *_: (bi * nt + i, 0)),
        pl.BlockSpec(memory_space=pl.ANY),
        pl.BlockSpec((1, d), lambda bi, i, *_: (0, 0)),
    ]
    args = [x, mod, wcol, out_buf, next_g.reshape(1, d)]
    if last:
        out_specs, out_shape = row_spec, jax.ShapeDtypeStruct((b, s, d), x.dtype)
    else:
        in_specs += [_mod_spec(d, 1), _mod_spec(d, 0)]
        args += [next_mod, next_mod]
        out_specs = (row_spec, row_spec)
        out_shape = (jax.ShapeDtypeStruct((b, s, d), x.dtype), jax.ShapeDtypeStruct((b, s, d), BF16))
    return pl.pallas_call(
        functools.partial(_combine_kernel, nt=nt, n_tok=b * s, last=last),
        grid_spec=pltpu.PrefetchScalarGridSpec(
            num_scalar_prefetch=1,
            grid=(b, nt),
            in_specs=in_specs,
            out_specs=out_specs,
            scratch_shapes=[pltpu.VMEM((2, TOP_K, tt, d), F32), pltpu.SemaphoreType.DMA((2,))],
        ),
        out_shape=out_shape,
        compiler_params=_params(("arbitrary", "arbitrary")),
        name="combine",
    )(dest, *args)


def _moe(x, g, mod, rw_pad, router_b, layer, w_gate, w_up, w_down, next_g, next_mod):
    b, s, d = x.shape
    n = b * s
    n_exp = router_b.shape[0]
    h_rows, meta, wcol, cnt = _router(x, g, mod, 4, 3, rw_pad, router_b)
    counts = cnt[:, 0]
    padded = (counts + MOE_BLOCK - 1) // MOE_BLOCK * MOE_BLOCK
    pad_end = jnp.cumsum(padded)
    start = pad_end - padded
    nb = (n * TOP_K + n_exp * (MOE_BLOCK - 1) + MOE_BLOCK - 1) // MOE_BLOCK
    blk_first = jnp.arange(nb, dtype=I32) * MOE_BLOCK
    blk_e = jnp.minimum(jnp.sum(blk_first[:, None] >= pad_end[None, :], axis=1), n_exp - 1).astype(I32)
    n_used = (pad_end[-1:] // MOE_BLOCK).astype(I32)
    is_e = meta[0:2, :, None] == jnp.arange(n_exp, dtype=I32)
    dest = (meta[2:4] + jnp.sum(jnp.where(is_e, start.astype(I32), 0), axis=-1)).reshape(-1)
    n_rows = nb * MOE_BLOCK
    fill_lo = jnp.concatenate([start + counts, pad_end[-1:]]).astype(I32)
    fill_hi = jnp.concatenate([pad_end, jnp.full((1,), n_rows, pad_end.dtype)]).astype(I32)
    buf = _dispatch(h_rows.reshape(n, d // 2), dest, fill_lo, fill_hi, n_rows)
    out_buf = _experts(buf, blk_e, n_used, layer, w_gate, w_up, w_down)
    return _combine(x, mod, 5, wcol, out_buf, dest, next_g, next_mod)


def _pad_cols(w, width):
    return jnp.pad(w, ((0, 0), (0, width - w.shape[1])))


def kernel(x, c, ada_w, ada_b, norm1, norm2, w_in_ab, fox_fb, kv_norm, w_uk, w_uv, w_out_ab, w_in_c, conv_w, w_out_c, rel_bias, router_w, router_b, exp_gate, exp_up, exp_down, final_norm):
    depth, d = norm1.shape
    b, s, _ = x.shape
    fox_heads = fox_fb.shape[1]
    dsa_heads, rank = w_uk.shape[1], w_uk.shape[2]
    fox_w, dsa_w = fox_heads * HEAD_DIM, dsa_heads * HEAD_DIM
    even_cols = w_in_ab.shape[2]
    idx_heads = (even_cols - 3 * fox_w - fox_heads - dsa_w - rank - IDX_DIM) // (IDX_DIM + 1)
    iq_w = idx_heads * IDX_DIM
    o_fa = 3 * fox_w
    o_qb = o_fa + fox_heads
    o_ckv = o_qb + dsa_w
    o_iq = o_ckv + rank
    o_ik = o_iq + iq_w
    o_iw = o_ik + IDX_DIM
    assert o_iw + idx_heads == even_cols and fox_heads <= LANES and idx_heads <= LANES

    mods = _ada_mod(c, ada_w, ada_b)
    rw_pad = _pad_cols(router_w, LANES).T.astype(BF16)
    experts = (exp_gate, exp_up, exp_down)
    mod_of = [mods[l].reshape(b, 1, 6 * d) for l in range(depth)]
    h = _norm_mod(x, norm1[0], mod_of[0], 1, 0)
    for l in range(depth):
        mod = mod_of[l]
        i = l // 2
        if l % 2 == 0:
            w = w_in_ab[i]
            w_a = jnp.concatenate([w[:, :o_fa], w[:, o_qb:o_ckv], w[:, o_iq:o_ik]], axis=1).astype(BF16)
            w_s = jnp.concatenate([w[:, o_ckv:o_iq], _pad_cols(w[:, o_fa:o_qb], LANES),
                                   _pad_cols(w[:, o_ik:o_iw], LANES), _pad_cols(w[:, o_iw:], LANES)],
                                  axis=1).astype(BF16)
            z_a = _matmul(h, w_a, BF16)
            z_s = _matmul(h, w_s, F32)
            ckv_n, neg_cum, ik_e, ik_o, iw_s = _even_prep(z_s, kv_norm[i], fox_fb[i], idx_heads)
            o_a = _fox_attention(z_a, neg_cum, fox_heads)
            o_b = _dsa_attention(z_a, iw_s, ckv_n, ik_e, ik_o, w_uk[i].astype(BF16), w_uv[i].astype(BF16),
                                 rel_bias, fox_w, idx_heads)
            w_o = w_out_ab[i].astype(BF16)
            x = _matmul_residual([(o_a, w_o[:fox_w]), (o_b, w_o[fox_w:])], x, mod, 2)
        else:
            y = _shortconv(h, w_in_c[i].astype(BF16), conv_w[i])
            x = _matmul_residual([(y, w_out_c[i].astype(BF16))], x, mod, 2)
        if l + 1 < depth:
            x, h = _moe(x, norm2[l], mod, rw_pad, router_b, l, *experts, norm1[l + 1], mod_of[l + 1])
        else:
            x = _moe(x, norm2[l], mod, rw_pad, router_b, l, *experts, final_norm, None)
    return x
```

```python
import functools
import math

import numpy as np
import jax
import jax.numpy as jnp
from jax import lax
from jax.experimental import pallas as pl
from jax.experimental.pallas import tpu as pltpu

F32 = jnp.float32
BF16 = jnp.bfloat16
I32 = jnp.int32
U32 = jnp.uint32

HEAD_DIM = 128
IDX_DIM = 64
TOPK_MAX = 256
N_BUCKETS = 32
MAX_DISTANCE = 128
N_GROUPS = 4
TOP_K = 2
EPS = 1e-6
LANES = 128
VMEM_LIMIT = 56 * 1024 * 1024
MOE_BLOCK = 256
EXPERT_STAGE_SLOTS = 3
COMBINE_ROWS = 64
DMA_ISSUE_UNROLL = 8
NT_DIMS = (((1,), (1,)), ((), ()))


def _params(sem, vmem=VMEM_LIMIT):
    return pltpu.CompilerParams(dimension_semantics=sem, vmem_limit_bytes=vmem)


def _ada_kernel(c_ref, w_ref, b_ref, o_ref):
    c = c_ref[...]
    ca = (c * jax.nn.sigmoid(c)).astype(BF16)
    o_ref[0] = jnp.dot(ca, w_ref[0].astype(BF16), preferred_element_type=F32) + b_ref[0]


def _ada_mod(c, ada_w, ada_b):
    depth, d, n6 = ada_w.shape
    b = c.shape[0]
    tn = min(512, n6)
    return pl.pallas_call(
        _ada_kernel,
        grid=(depth, n6 // tn),
        in_specs=[
            pl.BlockSpec((b, d), lambda l, j: (0, 0)),
            pl.BlockSpec((1, d, tn), lambda l, j: (l, 0, j)),
            pl.BlockSpec((1, 1, tn), lambda l, j: (l, 0, j)),
        ],
        out_specs=pl.BlockSpec((1, b, tn), lambda l, j: (l, 0, j)),
        out_shape=jax.ShapeDtypeStruct((depth, b, n6), F32),
        compiler_params=_params(("arbitrary", "arbitrary")),
        name="ada_mod",
    )(c, ada_w, ada_b.reshape(depth, 1, n6))


def _rms(x, g):
    return x * lax.rsqrt(jnp.mean(x * x, axis=-1, keepdims=True) + EPS) * g


def _norm_mod_kernel(x_ref, g_ref, sc_ref, sh_ref, o_ref):
    y = _rms(x_ref[0], g_ref[...])
    o_ref[0] = (y * (1.0 + sc_ref[0]) + sh_ref[0]).astype(o_ref.dtype)


def _mod_spec(d, chunk):
    return pl.BlockSpec((1, 1, d), lambda b, *_: (b, 0, chunk))


def _norm_mod(x, g, mod, sc_chunk, sh_chunk, ts=256):
    b, s, d = x.shape
    ts = min(ts, s)
    return pl.pallas_call(
        _norm_mod_kernel,
        grid=(b, s // ts),
        in_specs=[
            pl.BlockSpec((1, ts, d), lambda bi, i: (bi, i, 0)),
            pl.BlockSpec((1, d), lambda bi, i: (0, 0)),
            _mod_spec(d, sc_chunk),
            _mod_spec(d, sh_chunk),
        ],
        out_specs=pl.BlockSpec((1, ts, d), lambda bi, i: (bi, i, 0)),
        out_shape=jax.ShapeDtypeStruct((b, s, d), BF16),
        compiler_params=_params(("arbitrary", "arbitrary")),
        name="norm_mod",
    )(x, g.reshape(1, d), mod, mod)


def _mm_kernel(a_ref, w_ref, o_ref):
    o_ref[0] = jnp.dot(a_ref[0], w_ref[...], preferred_element_type=F32).astype(o_ref.dtype)


def _pick_tile(n, target):
    if n <= target:
        return n
    return max(t for t in range(LANES, target + 1, LANES) if n % t == 0)


def _matmul(a, w, out_dtype, tm=1024, tn=1024):
    b, s, k = a.shape
    n = w.shape[1]
    tm, tn = min(tm, s), _pick_tile(n, tn)
    return pl.pallas_call(
        _mm_kernel,
        grid=(b, s // tm, n // tn),
        in_specs=[
            pl.BlockSpec((1, tm, k), lambda bi, i, j: (bi, i, 0)),
            pl.BlockSpec((k, tn), lambda bi, i, j: (0, j)),
        ],
        out_specs=pl.BlockSpec((1, tm, tn), lambda bi, i, j: (bi, i, j)),
        out_shape=jax.ShapeDtypeStruct((b, s, n), out_dtype),
        compiler_params=_params(("arbitrary", "arbitrary", "arbitrary")),
        name="matmul",
    )(a, w)


def _mm_res_kernel(*refs, n_pairs):
    x_ref, g_ref, o_ref = refs[2 * n_pairs:]
    acc = None
    for p in range(n_pairs):
        part = jnp.dot(refs[2 * p][0], refs[2 * p + 1][...], preferred_element_type=F32)
        acc = part if acc is None else acc + part
    o_ref[0] = x_ref[0] + g_ref[0] * acc


def _matmul_residual(pairs, x, mod, g_chunk, tm=1024, tn=512):
    b, s, d = x.shape
    tm, tn = min(tm, s), min(tn, d)
    in_specs, args = [], []
    for a, w in pairs:
        k = a.shape[2]
        in_specs += [pl.BlockSpec((1, tm, k), lambda bi, i, j: (bi, i, 0)),
                     pl.BlockSpec((k, tn), lambda bi, i, j: (0, j))]
        args += [a, w]
    nj = d // tn
    in_specs += [pl.BlockSpec((1, tm, tn), lambda bi, i, j: (bi, i, j)),
                 pl.BlockSpec((1, 1, tn), lambda bi, i, j: (bi, 0, g_chunk * nj + j))]
    return pl.pallas_call(
        functools.partial(_mm_res_kernel, n_pairs=len(pairs)),
        grid=(b, s // tm, nj),
        in_specs=in_specs,
        out_specs=pl.BlockSpec((1, tm, tn), lambda bi, i, j: (bi, i, j)),
        out_shape=jax.ShapeDtypeStruct((b, s, d), x.dtype),
        compiler_params=_params(("arbitrary", "arbitrary", "arbitrary")),
        name="matmul_residual",
    )(*args, x, mod)


def _lane_cumsum(x):
    n = x.shape[-1]
    lane = lax.broadcasted_iota(I32, x.shape, x.ndim - 1)
    d = 1
    while d < n:
        x = x + jnp.where(lane >= d, pltpu.roll(x, d, x.ndim - 1), 0.0)
        d *= 2
    return x


def _even_prep_kernel(z_ref, kvn_ref, fb_ref, ckv_ref, negcum_ref, ike_ref, iko_ref, iw_ref,
                      *, rank, fox_heads, idx_heads):
    z = z_ref[0]
    ckv_ref[0] = _rms(z[:, :rank], kvn_ref[...]).astype(BF16)
    f_t = z[:, rank:rank + LANES].T
    logf = jax.nn.log_sigmoid(f_t[:fox_heads] + fb_ref[...])
    negcum_ref[0] = -_lane_cumsum(logf)
    ik = z[:, rank + LANES:rank + LANES + IDX_DIM].astype(BF16)
    zero = jnp.zeros_like(ik)
    ike_ref[0] = jnp.concatenate([ik, zero], axis=1)
    iko_ref[0] = jnp.concatenate([zero, ik], axis=1)
    iw_ref[0] = z[:, rank + 2 * LANES:rank + 2 * LANES + idx_heads] * (idx_heads ** -0.5) * (IDX_DIM ** -0.5)


def _even_prep(z_s, kv_norm, fox_fb, idx_heads):
    b, s, w = z_s.shape
    rank = kv_norm.shape[0]
    fh = fox_fb.shape[0]
    outs = (
        jax.ShapeDtypeStruct((b, s, rank), BF16),
        jax.ShapeDtypeStruct((b, fh, s), F32),
        jax.ShapeDtypeStruct((b, s, LANES), BF16),
        jax.ShapeDtypeStruct((b, s, LANES), BF16),
        jax.ShapeDtypeStruct((b, s, idx_heads), F32),
    )
    return pl.pallas_call(
        functools.partial(_even_prep_kernel, rank=rank, fox_heads=fh, idx_heads=idx_heads),
        grid=(b,),
        in_specs=[
            pl.BlockSpec((1, s, w), lambda bi: (bi, 0, 0)),
            pl.BlockSpec((1, rank), lambda bi: (0, 0)),
            pl.BlockSpec((fh, 1), lambda bi: (0, 0)),
        ],
        out_specs=(
            pl.BlockSpec((1, s, rank), lambda bi: (bi, 0, 0)),
            pl.BlockSpec((1, fh, s), lambda bi: (bi, 0, 0)),
            pl.BlockSpec((1, s, LANES), lambda bi: (bi, 0, 0)),
            pl.BlockSpec((1, s, LANES), lambda bi: (bi, 0, 0)),
            pl.BlockSpec((1, s, idx_heads), lambda bi: (bi, 0, 0)),
        ),
        out_shape=outs,
        compiler_params=_params(("arbitrary",)),
        name="even_prep",
    )(z_s, kv_norm.reshape(1, rank), fox_fb.reshape(fh, 1))


def _fox_kernel(q_ref, k_ref, v_ref, nc_ref, o_ref, *, heads, tq):
    s_len = q_ref.shape[1]
    scale = HEAD_DIM ** -0.5
    lower = lax.broadcasted_iota(I32, (tq, tq), 0) >= lax.broadcasted_iota(I32, (tq, tq), 1)
    head_cols = [slice(hh * HEAD_DIM, (hh + 1) * HEAD_DIM) for hh in range(heads)]

    def q_body(qi, _):
        q0 = pl.multiple_of(qi * tq, tq)
        qs = [q_ref[0, pl.ds(q0, tq), hs] for hs in head_cols]

        def k_step(kj, carries, diagonal):
            k0 = pl.multiple_of(kj * tq, tq)
            out = []
            for hh, hs in enumerate(head_cols):
                m, l, acc = carries[hh]
                k = k_ref[0, pl.ds(k0, tq), hs]
                v = v_ref[0, pl.ds(k0, tq), hs]
                s = lax.dot_general(qs[hh], k, NT_DIMS, preferred_element_type=F32) * scale
                s = s + nc_ref[0, hh, pl.ds(kj, 1), :]
                if diagonal:
                    s = jnp.where(lower, s, -jnp.inf)
                m_new = jnp.maximum(m, jnp.max(s, axis=-1, keepdims=True))
                alpha = jnp.exp(m - m_new)
                p = jnp.exp(s - m_new)
                l = alpha * l + jnp.sum(p, axis=-1, keepdims=True)
                acc = alpha * acc + jnp.dot(p.astype(BF16), v, preferred_element_type=F32)
                out.append((m_new, l, acc))
            return tuple(out)

        init = tuple((jnp.full((tq, 1), -jnp.inf, F32), jnp.zeros((tq, 1), F32),
                      jnp.zeros((tq, HEAD_DIM), F32)) for _ in head_cols)
        carries = lax.fori_loop(0, qi, functools.partial(k_step, diagonal=False), init)
        carries = k_step(qi, carries, diagonal=True)
        for (m, l, acc), hs in zip(carries, head_cols):
            o_ref[0, pl.ds(q0, tq), hs] = (acc / l).astype(o_ref.dtype)
        return 0

    lax.fori_loop(0, s_len // tq, q_body, 0)


def _fox_attention(z_a, neg_cum, fox_heads, heads_per_step=4, tq=512):
    b, s, _ = z_a.shape
    tq = tk = min(tq, s)
    hp = heads_per_step
    wblk = hp * HEAD_DIM
    nblk = fox_heads // hp
    nc = neg_cum.reshape(b, fox_heads, s // tk, tk)
    return pl.pallas_call(
        functools.partial(_fox_kernel, heads=hp, tq=tq),
        grid=(b, nblk),
        in_specs=[
            pl.BlockSpec((1, s, wblk), lambda bi, h: (bi, 0, h)),
            pl.BlockSpec((1, s, wblk), lambda bi, h: (bi, 0, nblk + h)),
            pl.BlockSpec((1, s, wblk), lambda bi, h: (bi, 0, 2 * nblk + h)),
            pl.BlockSpec((1, hp, s // tk, tk), lambda bi, h: (bi, h, 0, 0)),
        ],
        out_specs=pl.BlockSpec((1, s, wblk), lambda bi, h: (bi, 0, h)),
        out_shape=jax.ShapeDtypeStruct((b, s, fox_heads * HEAD_DIM), BF16),
        compiler_params=_params(("arbitrary", "arbitrary")),
        name="fox_attention",
    )(z_a, z_a, z_a, nc)


DSA_TQ = 256
INT_MIN = int(np.iinfo(np.int32).min)


def _t5_thresholds(max_d):
    max_exact = N_BUCKETS // 2
    d = np.arange(max_d)
    df = np.maximum(d, 1).astype(np.float32)
    large = max_exact + (np.log(df / np.float32(max_exact)) / np.float32(math.log(MAX_DISTANCE / max_exact))
                         * np.float32(N_BUCKETS - max_exact)).astype(np.int32)
    table = np.where(d < max_exact, d, np.minimum(large, N_BUCKETS - 1))
    assert np.all(np.diff(table) >= 0) and table[-1] == N_BUCKETS - 1
    return [int(np.argmax(table >= b)) for b in range(N_BUCKETS)]


def _bias_tile_kernel(rb_ref, o_ref, *, tq, thresholds):
    h = pl.program_id(0)
    a = lax.broadcasted_iota(I32, (tq, 2 * tq), 0)
    c = lax.broadcasted_iota(I32, (tq, 2 * tq), 1)
    d = tq + a - c
    val = jnp.full((tq, 2 * tq), rb_ref[0, h], F32)
    for b in range(1, N_BUCKETS):
        val = jnp.where(d >= thresholds[b], rb_ref[b, h], val)
    o_ref[0] = val


def _bias_tiles(rel_bias, tq):
    heads = rel_bias.shape[1]
    thresholds = _t5_thresholds(2 * tq)
    assert thresholds[N_BUCKETS - 1] <= tq + 1
    return pl.pallas_call(
        functools.partial(_bias_tile_kernel, tq=tq, thresholds=thresholds),
        grid=(heads,),
        in_specs=[pl.BlockSpec(memory_space=pltpu.SMEM)],
        out_specs=pl.BlockSpec((1, tq, 2 * tq), lambda h: (h, 0, 0)),
        out_shape=jax.ShapeDtypeStruct((heads, tq, 2 * tq), F32),
        compiler_params=_params(("arbitrary",)),
        name="bias_tiles",
    )(rel_bias)


def _dsa_kernel(rb_ref, qb_ref, iq_ref, iw_ref, ckv_ref, ike_ref, iko_ref, wuk_ref, wuv_ref, tile_ref,
                o_ref, key_sc, madd_sc, p_sc, *, heads, idx_heads, topk, tq, group, i):
    s_len, rank = ckv_ref.shape[1], ckv_ref.shape[2]
    scale = HEAD_DIM ** -0.5

    iq = iq_ref[0]
    iw = iw_ref[0]
    ike = ike_ref[0]
    iko = iko_ref[0]
    nch = idx_heads * IDX_DIM // LANES
    score = jnp.zeros((tq, s_len), F32)
    for g0 in range(0, nch, 2):
        cs = list(range(g0, min(g0 + 2, nch)))
        a = jnp.concatenate([iq[:, c * LANES:(c + 1) * LANES] for c in cs], axis=0)
        r_e = lax.dot_general(a, ike, NT_DIMS, preferred_element_type=F32)
        r_o = lax.dot_general(a, iko, NT_DIMS, preferred_element_type=F32)
        for n, c in enumerate(cs):
            rows = slice(n * tq, (n + 1) * tq)
            score = score + jnp.maximum(r_e[rows], 0.0) * iw[:, 2 * c:2 * c + 1]
            score = score + jnp.maximum(r_o[rows], 0.0) * iw[:, 2 * c + 1:2 * c + 2]

    t_idx = i * tq + lax.broadcasted_iota(I32, (tq, s_len), 0)
    s_idx = lax.broadcasted_iota(I32, (tq, s_len), 1)
    bits = pltpu.bitcast(score, I32)
    key = bits ^ ((bits >> 31) & 0x7FFFFFFF)
    key_sc[...] = jnp.where(t_idx >= s_idx, key, INT_MIN)

    def bit_step(it, lo):
        cand = lo + lax.shift_left(jnp.int32(1), 31 - it)
        cnt = jnp.sum(jnp.where(key_sc[...] >= cand, 1.0, 0.0), axis=-1, keepdims=True)
        return jnp.where(cnt >= topk, cand, lo)
    thr = lax.fori_loop(0, 32, bit_step, jnp.full((tq, 1), INT_MIN, I32))

    n_gt = jnp.sum(jnp.where(key_sc[...] > thr, 1.0, 0.0), axis=-1, keepdims=True)
    n_eq = jnp.sum(jnp.where(key_sc[...] == thr, 1.0, 0.0), axis=-1, keepdims=True)
    need = topk - n_gt
    nbits = (s_len - 1).bit_length()

    def idx_step(it, p):
        cand = p + lax.shift_left(jnp.int32(1), nbits - 1 - it)
        tied_below = jnp.where(key_sc[...] == thr, jnp.where(s_idx < cand, 1.0, 0.0), 0.0)
        cnt = jnp.sum(tied_below, axis=-1, keepdims=True)
        return jnp.where(cnt < need, cand, p)

    excess = jnp.where(jnp.logical_and(n_eq > need, thr != INT_MIN), 1, 0)
    p_last = lax.cond(jnp.max(excess) > 0,
                      lambda: lax.fori_loop(0, nbits, idx_step, jnp.zeros((tq, 1), I32)),
                      lambda: jnp.full((tq, 1), s_len, I32))

    key = key_sc[...]
    keep = jnp.where(key > thr, 0.0, jnp.where(key == thr, jnp.where(s_idx <= p_last, 0.0, -jnp.inf), -jnp.inf))
    madd_sc[...] = jnp.where(t_idx >= s_idx, keep, -jnp.inf)

    qb = qb_ref[0]
    ckv = ckv_ref[0]
    madd = madd_sc[...]
    n_far = max(s_len - 2 * tq, 0)
    for g in range(heads // group):
        hs = range(g * group, (g + 1) * group)
        qg = jnp.concatenate(
            [lax.dot_general(qb[:, h * HEAD_DIM:(h + 1) * HEAD_DIM], wuk_ref[h], NT_DIMS,
                             preferred_element_type=F32).astype(BF16) for h in hs], axis=0)
        sg = lax.dot_general(qg, ckv, NT_DIMS, preferred_element_type=F32) * scale
        sums = []
        for n, h in enumerate(hs):
            near = tile_ref[h][:, 2 * tq - (s_len - n_far):]
            if n_far:
                far = jnp.full((tq, n_far), rb_ref[N_BUCKETS - 1, h], F32)
                near = jnp.concatenate([far, near], axis=1)
            s = sg[n * tq:(n + 1) * tq] + near + madd
            m = jnp.max(s, axis=-1, keepdims=True)
            p = jnp.exp(s - m)
            sums.append(jnp.sum(p, axis=-1, keepdims=True))
            p_sc[n] = p.astype(BF16)
        og = jnp.dot(p_sc[...].reshape(group * tq, s_len), ckv, preferred_element_type=F32)
        for n, h in enumerate(hs):
            ol = (og[n * tq:(n + 1) * tq] / sums[n]).astype(BF16)
            o_ref[0, :, h * HEAD_DIM:(h + 1) * HEAD_DIM] = jnp.dot(
                ol, wuv_ref[h], preferred_element_type=F32).astype(o_ref.dtype)


def _dsa_attention(z_a, iw_s, ckv_n, ik_e, ik_o, w_uk, w_uv, rel_bias, fox_w, idx_heads, group=2):
    b, s, _ = z_a.shape
    heads, rank, _ = w_uk.shape
    tq = min(DSA_TQ, s)
    group = min(group, heads)
    assert s & (s - 1) == 0 and heads % group == 0
    dsa_w = heads * HEAD_DIM
    iq_w = idx_heads * IDX_DIM
    assert (3 * fox_w) % dsa_w == 0 and (3 * fox_w + dsa_w) % iq_w == 0
    qb_blk = 3 * fox_w // dsa_w
    iq_blk = (3 * fox_w + dsa_w) // iq_w
    topk = min(TOPK_MAX, s // 4)
    tiles = _bias_tiles(rel_bias, tq)
    outs = []
    for i in range(s // tq):
        sk = (i + 1) * tq
        const = dict(pipeline_mode=pl.Buffered(1))
        outs.append(pl.pallas_call(
            functools.partial(_dsa_kernel, heads=heads, idx_heads=idx_heads, topk=topk, tq=tq, group=group, i=i),
            grid=(b,),
            in_specs=[
                pl.BlockSpec(memory_space=pltpu.SMEM),
                pl.BlockSpec((1, tq, dsa_w), lambda bi, i=i: (bi, i, qb_blk)),
                pl.BlockSpec((1, tq, iq_w), lambda bi, i=i: (bi, i, iq_blk)),
                pl.BlockSpec((1, tq, idx_heads), lambda bi, i=i: (bi, i, 0)),
                pl.BlockSpec((1, sk, rank), lambda bi: (bi, 0, 0)),
                pl.BlockSpec((1, sk, LANES), lambda bi: (bi, 0, 0)),
                pl.BlockSpec((1, sk, LANES), lambda bi: (bi, 0, 0)),
                pl.BlockSpec((heads, rank, HEAD_DIM), lambda bi: (0, 0, 0), **const),
                pl.BlockSpec((heads, rank, HEAD_DIM), lambda bi: (0, 0, 0), **const),
                pl.BlockSpec((heads, tq, 2 * tq), lambda bi: (0, 0, 0), **const),
            ],
            out_specs=pl.BlockSpec((1, tq, dsa_w), lambda bi: (bi, 0, 0)),
            out_shape=jax.ShapeDtypeStruct((b, tq, dsa_w), BF16),
            scratch_shapes=[
                pltpu.VMEM((tq, sk), I32),
                pltpu.VMEM((tq, sk), F32),
                pltpu.VMEM((group, tq, sk), BF16),
            ],
            compiler_params=_params(("arbitrary",)),
            name=f"dsa_attention_{i}",
        )(rel_bias, z_a, z_a, iw_s, ckv_n, ik_e, ik_o, w_uk, w_uv, tiles))
    return jnp.concatenate(outs, axis=1)


def _shortconv_kernel(a_ref, wb_ref, wc_ref, wu_ref, cw_ref, o_ref):
    a = a_ref[0]
    bg = jnp.dot(a, wb_ref[...], preferred_element_type=F32)
    cg = jnp.dot(a, wc_ref[...], preferred_element_type=F32)
    u = jnp.dot(a, wu_ref[...], preferred_element_type=F32)
    p = cg * u
    row = lax.broadcasted_iota(I32, p.shape, 0)
    p1 = jnp.where(row >= 1, pltpu.roll(p, 1, 0), 0.0)
    p2 = jnp.where(row >= 2, pltpu.roll(p, 2, 0), 0.0)
    cw = cw_ref[...]
    y = p2 * cw[0:1] + p1 * cw[1:2] + p * cw[2:3]
    o_ref[0] = (bg * y).astype(o_ref.dtype)


def _shortconv(h, w_in, conv_w, tn=256):
    b, s, d = h.shape
    assert conv_w.shape[0] == 3
    tn = min(tn, d)
    nj = d // tn
    return pl.pallas_call(
        _shortconv_kernel,
        grid=(b, nj),
        in_specs=[
            pl.BlockSpec((1, s, d), lambda bi, j: (bi, 0, 0), pipeline_mode=pl.Buffered(1)),
            pl.BlockSpec((d, tn), lambda bi, j: (0, j)),
            pl.BlockSpec((d, tn), lambda bi, j: (0, nj + j)),
            pl.BlockSpec((d, tn), lambda bi, j: (0, 2 * nj + j)),
            pl.BlockSpec((3, tn), lambda bi, j: (0, j)),
        ],
        out_specs=pl.BlockSpec((1, s, tn), lambda bi, j: (bi, 0, j)),
        out_shape=jax.ShapeDtypeStruct((b, s, d), BF16),
        compiler_params=_params(("arbitrary", "arbitrary")),
        name="shortconv",
    )(h, w_in, w_in, w_in, conv_w)


def _pack_bf16_pairs(xb):
    half = xb.shape[1] // 2
    bits = pltpu.bitcast(xb.astype(F32), U32)
    return (bits[:, half:] & jnp.uint32(0xFFFF0000)) | (bits[:, :half] >> 16)


def _unpack_bf16_pairs(words):
    lo = pltpu.bitcast(words << 16, F32).astype(BF16)
    hi = pltpu.bitcast(words & jnp.uint32(0xFFFF0000), F32).astype(BF16)
    return jnp.concatenate([lo, hi], axis=1)


def _router_kernel(x_ref, g_ref, sc_ref, sh_ref, rw_ref, rb_ref, h_ref, meta_ref, wcol_ref, cnt_ref,
                   base_sc, *, n_exp):
    epg = n_exp // N_GROUPS
    first = jnp.logical_and(pl.program_id(0) == 0, pl.program_id(1) == 0)

    @pl.when(first)
    def _():
        base_sc[...] = jnp.zeros_like(base_sc)

    h = _rms(x_ref[0], g_ref[...]) * (1.0 + sc_ref[0]) + sh_ref[0]
    t, d = h.shape
    hb = h.astype(BF16)
    h_ref[0] = _pack_bf16_pairs(hb)

    logits_t = lax.dot_general(rw_ref[...], hb, NT_DIMS, preferred_element_type=F32)
    aff = jax.nn.sigmoid(logits_t[:n_exp])
    sel = aff + rb_ref[...]
    srow = [sel[e:e + 1] for e in range(n_exp)]
    arow = [aff[e:e + 1] for e in range(n_exp)]

    gscore = []
    for g in range(N_GROUPS):
        a0, a1, a2, a3 = srow[g * epg:(g + 1) * epg]
        hi1, lo1 = jnp.maximum(a0, a1), jnp.minimum(a0, a1)
        hi2, lo2 = jnp.maximum(a2, a3), jnp.minimum(a2, a3)
        gscore.append(jnp.maximum(hi1, hi2) + jnp.maximum(jnp.minimum(hi1, hi2), jnp.maximum(lo1, lo2)))
    grp = jnp.zeros((1, t), I32)
    best = gscore[0]
    for g in range(1, N_GROUPS):
        take = gscore[g] > best
        grp = jnp.where(take, g, grp)
        best = jnp.where(take, gscore[g], best)

    def pick(rows_, idx, n):
        out = rows_[0]
        for k in range(1, n):
            out = jnp.where(idx == k, rows_[k], out)
        return out

    in_s = [pick([srow[g * epg + k] for g in range(N_GROUPS)], grp, N_GROUPS) for k in range(epg)]
    in_a = [pick([arow[g * epg + k] for g in range(N_GROUPS)], grp, N_GROUPS) for k in range(epg)]

    i1 = jnp.zeros((1, t), I32)
    v1 = in_s[0]
    for k in range(1, epg):
        take = in_s[k] > v1
        i1 = jnp.where(take, k, i1)
        v1 = jnp.where(take, in_s[k], v1)
    i2 = jnp.where(i1 == 0, 1, 0)
    v2 = jnp.where(i1 == 0, in_s[1], in_s[0])
    for k in range(1, epg):
        take = jnp.logical_and(i1 != k, jnp.logical_and(i2 != k, in_s[k] > v2))
        i2 = jnp.where(take, k, i2)
        v2 = jnp.where(take, in_s[k], v2)
    e1 = grp * epg + i1
    e2 = grp * epg + i2
    a1 = pick(in_a, i1, epg)
    a2 = pick(in_a, i2, epg)
    denom = a1 + a2
    w1 = a1 / denom
    w2 = a2 / denom

    eiota = lax.broadcasted_iota(I32, (n_exp, t), 0)
    oh1 = jnp.where(eiota == e1, 1.0, 0.0)
    oh2 = jnp.where(eiota == e2, 1.0, 0.0)
    oh = oh1 + oh2
    before = lax.broadcasted_iota(I32, (t, t), 0) < lax.broadcasted_iota(I32, (t, t), 1)
    excl = jnp.dot(oh.astype(BF16), jnp.where(before, 1.0, 0.0).astype(BF16), preferred_element_type=F32)
    rank = base_sc[:, 0:1] + excl
    r1 = jnp.sum(oh1 * rank, axis=0, keepdims=True).astype(I32)
    r2 = jnp.sum(oh2 * rank, axis=0, keepdims=True).astype(I32)
    base_sc[...] = base_sc[...] + jnp.sum(oh, axis=1, keepdims=True)

    meta_ref[...] = jnp.concatenate([e1, e2, r1, r2, jnp.zeros((4, t), I32)], axis=0)
    wrows = jnp.concatenate([w1, w2, jnp.zeros((LANES - 2, t), F32)], axis=0)
    wcol_ref[...] = wrows.T
    cnt_ref[...] = base_sc[...].astype(I32)


def _router(x, g, mod, sc_chunk, sh_chunk, rw_pad, router_b, tt=256):
    b, s, d = x.shape
    n_exp = router_b.shape[0]
    assert n_exp // N_GROUPS == 4 and TOP_K == 2
    tt = min(tt, s)
    nt = s // tt
    n = b * s
    outs = (
        jax.ShapeDtypeStruct((b, s, d // 2), U32),
        jax.ShapeDtypeStruct((8, n), I32),
        jax.ShapeDtypeStruct((n, LANES), F32),
        jax.ShapeDtypeStruct((n_exp, LANES), I32),
    )
    return pl.pallas_call(
        functools.partial(_router_kernel, n_exp=n_exp),
        grid=(b, nt),
        in_specs=[
            pl.BlockSpec((1, tt, d), lambda bi, i: (bi, i, 0)),
            pl.BlockSpec((1, d), lambda bi, i: (0, 0)),
            _mod_spec(d, sc_chunk),
            _mod_spec(d, sh_chunk),
            pl.BlockSpec((LANES, d), lambda bi, i: (0, 0)),
            pl.BlockSpec((n_exp, 1), lambda bi, i: (0, 0)),
        ],
        out_specs=(
            pl.BlockSpec((1, tt, d // 2), lambda bi, i: (bi, i, 0)),
            pl.BlockSpec((8, tt), lambda bi, i: (0, bi * nt + i)),
            pl.BlockSpec((tt, LANES), lambda bi, i: (bi * nt + i, 0)),
            pl.BlockSpec((n_exp, LANES), lambda bi, i: (0, 0)),
        ),
        out_shape=outs,
        scratch_shapes=[pltpu.VMEM((n_exp, LANES), F32)],
        compiler_params=_params(("arbitrary", "arbitrary")),
        name="router",
    )(x, g.reshape(1, d), mod, mod, rw_pad, router_b.reshape(n_exp, 1))


def _row(ref, r):
    return ref.at[pl.ds(r, 1), :]


def _dispatch_kernel(dest_ref, fill_lo_ref, fill_hi_ref, h_ref, buf_hbm, zero_sc, sem, *, n_tok, n_exp):
    chunk = h_ref.shape[0]
    ci = pl.program_id(0)

    @pl.when(ci == 0)
    def _():
        zero_sc[...] = jnp.zeros_like(zero_sc)
        for e in range(n_exp + 1):
            lo, hi = fill_lo_ref[e], fill_hi_ref[e]

            def zfill(r, _):
                pltpu.make_async_copy(_row(zero_sc, 0), _row(buf_hbm, r), sem).start()
                return 0
            lax.fori_loop(lo, hi, zfill, 0)

            def zdrain(r, _):
                pltpu.make_async_copy(_row(zero_sc, 0), _row(buf_hbm, 0), sem).wait()
                return 0
            lax.fori_loop(lo, hi, zdrain, 0)

    def issue(t, _):
        for k in range(TOP_K):
            a = k * n_tok + ci * chunk + t
            pltpu.make_async_copy(_row(h_ref, t), _row(buf_hbm, dest_ref[a]), sem).start()
        return 0
    lax.fori_loop(0, chunk, issue, 0, unroll=DMA_ISSUE_UNROLL)

    def drain(t, _):
        for k in range(TOP_K):
            pltpu.make_async_copy(_row(h_ref, 0), _row(buf_hbm, 0), sem).wait()
        return 0
    lax.fori_loop(0, chunk, drain, 0)


def _dispatch(h_rows, dest, fill_lo, fill_hi, n_rows, chunk=256):
    n, w = h_rows.shape
    n_exp = fill_lo.shape[0] - 1
    chunk = min(chunk, n)
    return pl.pallas_call(
        functools.partial(_dispatch_kernel, n_tok=n, n_exp=n_exp),
        grid_spec=pltpu.PrefetchScalarGridSpec(
            num_scalar_prefetch=3,
            grid=(n // chunk,),
            in_specs=[pl.BlockSpec((chunk, w), lambda i, *_: (i, 0))],
            out_specs=pl.BlockSpec(memory_space=pl.ANY),
            scratch_shapes=[pltpu.VMEM((8, w), h_rows.dtype), pltpu.SemaphoreType.DMA(())],
        ),
        out_shape=jax.ShapeDtypeStruct((n_rows, w), h_rows.dtype),
        compiler_params=_params(("arbitrary",)),
        name="dispatch",
    )(dest, fill_lo, fill_hi, h_rows)


def _expert_kernel(blk_e_ref, nused_ref, x_ref, wg_hbm, wu_hbm, wd_hbm, o_ref,
                   wg_sc, wu_sc, wd_sc, stage_in, stage_out, sem, *, layer):
    bi = pl.program_id(0)
    e = blk_e_ref[bi]
    active = bi < nused_ref[0]
    new_expert = jnp.logical_or(bi == 0, e != blk_e_ref[jnp.maximum(bi - 1, 0)])

    @pl.when(jnp.logical_and(active, new_expert))
    def _():
        n_slots = stage_in.shape[0]
        copies = []
        for src, dst, stage in ((wg_hbm, wg_sc, stage_in), (wu_hbm, wu_sc, stage_in), (wd_hbm, wd_sc, stage_out)):
            rows = stage.shape[1]
            for c in range(dst.shape[0] // rows):
                slot = len(copies) % n_slots
                cp = pltpu.make_async_copy(src.at[layer, e, pl.ds(c * rows, rows), :], stage.at[slot], sem.at[slot])
                copies.append((cp, stage, slot, dst, c * rows, rows))
        for cp, *_ in copies[:n_slots - 1]:
            cp.start()
        for n, (cp, stage, slot, dst, r0, rows) in enumerate(copies):
            if n + n_slots - 1 < len(copies):
                copies[n + n_slots - 1][0].start()
            cp.wait()
            dst[r0:r0 + rows, :] = stage[slot].astype(BF16)

    @pl.when(active)
    def _():
        xs = _unpack_bf16_pairs(x_ref[...])
        gate = jnp.dot(xs, wg_sc[...], preferred_element_type=F32)
        up = jnp.dot(xs, wu_sc[...], preferred_element_type=F32)
        hid = (jax.nn.silu(gate) * up).astype(BF16)
        o_ref[...] = jnp.dot(hid, wd_sc[...], preferred_element_type=F32)

    @pl.when(jnp.logical_not(active))
    def _():
        o_ref[...] = jnp.zeros_like(o_ref)


def _experts(buf, blk_e, n_used, layer, w_gate, w_up, w_down):
    n_rows, half = buf.shape
    _, n_exp, d, f = w_gate.shape
    nb = n_rows // MOE_BLOCK
    stage_elems = 512 * 1024
    rows_in, rows_out = min(d, stage_elems // f), min(f, stage_elems // d)
    assert d % rows_in == 0 and f % rows_out == 0

    def row_map(bi, blk_e_ref, nused_ref):
        return (jnp.minimum(bi, nused_ref[0] - 1), 0)

    return pl.pallas_call(
        functools.partial(_expert_kernel, layer=layer),
        grid_spec=pltpu.PrefetchScalarGridSpec(
            num_scalar_prefetch=2,
            grid=(nb,),
            in_specs=[
                pl.BlockSpec((MOE_BLOCK, half), row_map),
                pl.BlockSpec(memory_space=pl.ANY),
                pl.BlockSpec(memory_space=pl.ANY),
                pl.BlockSpec(memory_space=pl.ANY),
            ],
            out_specs=pl.BlockSpec((MOE_BLOCK, d), lambda bi, *_: (bi, 0)),
            scratch_shapes=[
                pltpu.VMEM((d, f), BF16),
                pltpu.VMEM((d, f), BF16),
                pltpu.VMEM((f, d), BF16),
                pltpu.VMEM((EXPERT_STAGE_SLOTS, rows_in, f), F32),
                pltpu.VMEM((EXPERT_STAGE_SLOTS, rows_out, d), F32),
                pltpu.SemaphoreType.DMA((EXPERT_STAGE_SLOTS,)),
            ],
        ),
        out_shape=jax.ShapeDtypeStruct((n_rows, d), F32),
        compiler_params=_params(("arbitrary",)),
        name="experts",
    )(blk_e, n_used, buf, w_gate, w_up, w_down)


def _combine_kernel(dest_ref, x_ref, g_ref, wcol_ref, ob_hbm, ng_ref, *rest, nt, n_tok, last):
    if last:
        o_ref, gbuf, sem = rest
    else:
        sc_ref, sh_ref, o_ref, h_ref, gbuf, sem = rest
    tt = x_ref.shape[1]
    n_steps = pl.num_programs(0) * nt
    step = pl.program_id(0) * nt + pl.program_id(1)
    slot = step % 2
    nxt = jnp.minimum(step + 1, n_steps - 1)

    def issue(step_idx, slot_idx, t):
        for k in range(TOP_K):
            a = k * n_tok + step_idx * tt + t
            pltpu.make_async_copy(_row(ob_hbm, dest_ref[a]), _row(gbuf.at[slot_idx, k], t),
                                  sem.at[slot_idx]).start()

    def drain(slot_idx):
        def body(t, _):
            for k in range(TOP_K):
                pltpu.make_async_copy(_row(ob_hbm, 0), _row(gbuf.at[0, 0], 0), sem.at[slot_idx]).wait()
            return 0
        lax.fori_loop(0, tt, body, 0)

    @pl.when(step == 0)
    def _():
        def first(t, _):
            issue(0, 0, t)
            return 0
        lax.fori_loop(0, tt, first, 0, unroll=DMA_ISSUE_UNROLL)

    drain(slot)

    def chunk(c, _):
        r0 = pl.multiple_of(c * COMBINE_ROWS, COMBINE_ROWS)
        for u in range(COMBINE_ROWS):
            issue(nxt, 1 - slot, r0 + u)
        rows = pl.ds(r0, COMBINE_ROWS)
        y = gbuf[slot, 0, rows, :] * wcol_ref[rows, 0:1] + gbuf[slot, 1, rows, :] * wcol_ref[rows, 1:2]
        x_new = x_ref[0, rows, :] + g_ref[0] * y
        if last:
            o_ref[0, rows, :] = _rms(x_new, ng_ref[...])
        else:
            o_ref[0, rows, :] = x_new
            h_ref[0, rows, :] = (_rms(x_new, ng_ref[...]) * (1.0 + sc_ref[0]) + sh_ref[0]).astype(h_ref.dtype)
        return 0
    lax.fori_loop(0, tt // COMBINE_ROWS, chunk, 0)

    @pl.when(step == n_steps - 1)
    def _():
        drain(1 - slot)


def _combine(x, mod, g_chunk, wcol, out_buf, dest, next_g, next_mod, tt=256):
    b, s, d = x.shape
    tt = min(tt, s)
    nt = s // tt
    last = next_mod is None
    row_spec = pl.BlockSpec((1, tt, d), lambda bi, i, *_: (bi, i, 0))
    in_specs = [
        row_spec,
        pl.BlockSpec((1, 1, d), lambda bi, i, *_: (bi, 0, g_chunk)),
        pl.BlockSpec((tt, LANES), lambda bi, i, *_: (bi * nt + i, 0)),
        pl.BlockSpec(memory_space=pl.ANY),
        pl.BlockSpec((1, d), lambda bi, i, *_: (0, 0)),
    ]
    args = [x, mod, wcol, out_buf, next_g.reshape(1, d)]
    if last:
        out_specs, out_shape = row_spec, jax.ShapeDtypeStruct((b, s, d), x.dtype)
    else:
        in_specs += [_mod_spec(d, 1), _mod_spec(d, 0)]
        args += [next_mod, next_mod]
        out_specs = (row_spec, row_spec)
        out_shape = (jax.ShapeDtypeStruct((b, s, d), x.dtype), jax.ShapeDtypeStruct((b, s, d), BF16))
    return pl.pallas_call(
        functools.partial(_combine_kernel, nt=nt, n_tok=b * s, last=last),
        grid_spec=pltpu.PrefetchScalarGridSpec(
            num_scalar_prefetch=1,
            grid=(b, nt),
            in_specs=in_specs,
            out_specs=out_specs,
            scratch_shapes=[pltpu.VMEM((2, TOP_K, tt, d), F32), pltpu.SemaphoreType.DMA((2,))],
        ),
        out_shape=out_shape,
        compiler_params=_params(("arbitrary", "arbitrary")),
        name="combine",
    )(dest, *args)


def _moe(x, g, mod, rw_pad, router_b, layer, w_gate, w_up, w_down, next_g, next_mod):
    b, s, d = x.shape
    n = b * s
    n_exp = router_b.shape[0]
    h_rows, meta, wcol, cnt = _router(x, g, mod, 4, 3, rw_pad, router_b)
    counts = cnt[:, 0]
    padded = (counts + MOE_BLOCK - 1) // MOE_BLOCK * MOE_BLOCK
    pad_end = jnp.cumsum(padded)
    start = pad_end - padded
    nb = (n * TOP_K + n_exp * (MOE_BLOCK - 1) + MOE_BLOCK - 1) // MOE_BLOCK
    blk_first = jnp.arange(nb, dtype=I32) * MOE_BLOCK
    blk_e = jnp.minimum(jnp.sum(blk_first[:, None] >= pad_end[None, :], axis=1), n_exp - 1).astype(I32)
    n_used = (pad_end[-1:] // MOE_BLOCK).astype(I32)
    is_e = meta[0:2, :, None] == jnp.arange(n_exp, dtype=I32)
    dest = (meta[2:4] + jnp.sum(jnp.where(is_e, start.astype(I32), 0), axis=-1)).reshape(-1)
    n_rows = nb * MOE_BLOCK
    fill_lo = jnp.concatenate([start + counts, pad_end[-1:]]).astype(I32)
    fill_hi = jnp.concatenate([pad_end, jnp.full((1,), n_rows, pad_end.dtype)]).astype(I32)
    buf = _dispatch(h_rows.reshape(n, d // 2), dest, fill_lo, fill_hi, n_rows)
    out_buf = _experts(buf, blk_e, n_used, layer, w_gate, w_up, w_down)
    return _combine(x, mod, 5, wcol, out_buf, dest, next_g, next_mod)


def _pad_cols(w, width):
    return jnp.pad(w, ((0, 0), (0, width - w.shape[1])))


def kernel(x, c, ada_w, ada_b, norm1, norm2, w_in_ab, fox_fb, kv_norm, w_uk, w_uv, w_out_ab, w_in_c, conv_w, w_out_c, rel_bias, router_w, router_b, exp_gate, exp_up, exp_down, final_norm):
    depth, d = norm1.shape
    b, s, _ = x.shape
    fox_heads = fox_fb.shape[1]
    dsa_heads, rank = w_uk.shape[1], w_uk.shape[2]
    fox_w, dsa_w = fox_heads * HEAD_DIM, dsa_heads * HEAD_DIM
    even_cols = w_in_ab.shape[2]
    idx_heads = (even_cols - 3 * fox_w - fox_heads - dsa_w - rank - IDX_DIM) // (IDX_DIM + 1)
    iq_w = idx_heads * IDX_DIM
    o_fa = 3 * fox_w
    o_qb = o_fa + fox_heads
    o_ckv = o_qb + dsa_w
    o_iq = o_ckv + rank
    o_ik = o_iq + iq_w
    o_iw = o_ik + IDX_DIM
    assert o_iw + idx_heads == even_cols and fox_heads <= LANES and idx_heads <= LANES

    mods = _ada_mod(c, ada_w, ada_b)
    rw_pad = _pad_cols(router_w, LANES).T.astype(BF16)
    experts = (exp_gate, exp_up, exp_down)
    mod_of = [mods[l].reshape(b, 1, 6 * d) for l in range(depth)]
    h = _norm_mod(x, norm1[0], mod_of[0], 1, 0)
    for l in range(depth):
        mod = mod_of[l]
        i = l // 2
        if l % 2 == 0:
            w = w_in_ab[i]
            w_a = jnp.concatenate([w[:, :o_fa], w[:, o_qb:o_ckv], w[:, o_iq:o_ik]], axis=1).astype(BF16)
            w_s = jnp.concatenate([w[:, o_ckv:o_iq], _pad_cols(w[:, o_fa:o_qb], LANES),
                                   _pad_cols(w[:, o_ik:o_iw], LANES), _pad_cols(w[:, o_iw:], LANES)],
                                  axis=1).astype(BF16)
            z_a = _matmul(h, w_a, BF16)
            z_s = _matmul(h, w_s, F32)
            ckv_n, neg_cum, ik_e, ik_o, iw_s = _even_prep(z_s, kv_norm[i], fox_fb[i], idx_heads)
            o_a = _fox_attention(z_a, neg_cum, fox_heads)
            o_b = _dsa_attention(z_a, iw_s, ckv_n, ik_e, ik_o, w_uk[i].astype(BF16), w_uv[i].astype(BF16),
                                 rel_bias, fox_w, idx_heads)
            w_o = w_out_ab[i].astype(BF16)
            x = _matmul_residual([(o_a, w_o[:fox_w]), (o_b, w_o[fox_w:])], x, mod, 2)
        else:
            y = _shortconv(h, w_in_c[i].astype(BF16), conv_w[i])
            x = _matmul_residual([(y, w_out_c[i].astype(BF16))], x, mod, 2)
        if l + 1 < depth:
            x, h = _moe(x, norm2[l], mod, rw_pad, router_b, l, *experts, norm1[l + 1], mod_of[l + 1])
        else:
            x = _moe(x, norm2[l], mod, rw_pad, router_b, l, *experts, final_norm, None)
    return x
```
